```python
import jax, jax.numpy as jnp
from jax import lax
import numpy as np

D_MODEL = 2048
BATCH = 4
SEQ = 4096
DEPTH = 2

HEAD_DIM = 128
A_HEADS = 6
B_HEADS = 5
C_HEADS = 5
A_WIDTH = A_HEADS * HEAD_DIM
B_WIDTH = B_HEADS * HEAD_DIM
C_WIDTH = C_HEADS * HEAD_DIM
MIX_WIDTH = A_WIDTH + B_WIDTH + C_WIDTH
N_BRANCHES = 3
MOBA_BLOCK = 256
MOBA_TOPK = 3
MOBA_Q_CHUNK = 32
ROPE_THETA = 500000.0
ROPE_DIM = HEAD_DIM // 4
RET_THETA = 10000.0
RET_CHUNK = 128
GDN_CHUNK = 64
CONV_WIDTH = 4
FFN_HIDDEN = -(-8 * D_MODEL // (3 * 256)) * 256
NORM_EPS = 1e-6
SPLIT_SIZES = (A_WIDTH,) * 3 + (B_WIDTH,) * 4 + (C_WIDTH,) * 4 + (C_HEADS, C_HEADS, N_BRANCHES * D_MODEL)
IN_COLS = sum(SPLIT_SIZES)

kernel_name = 'hybrid_moba_retention_gdn_gated_block'


def rmsnorm(x, g):
    xf = x.astype(jnp.float32)
    xf = xf * lax.rsqrt(jnp.mean(xf * xf, axis=-1, keepdims=True) + NORM_EPS)
    return xf.astype(x.dtype) * g


def l2norm(x):
    return x * lax.rsqrt(jnp.sum(x * x, axis=-1, keepdims=True) + NORM_EPS)


def split_heads(t, n):
    b, s, _ = t.shape
    return t.reshape(b, s, n, HEAD_DIM).transpose(0, 2, 1, 3)


def merge_heads(t):
    b, h, s, d = t.shape
    return t.transpose(0, 2, 1, 3).reshape(b, s, h * d)


def apply_rotary(x, inv_freq):
    rot = 2 * inv_freq.shape[0]
    pos = jnp.arange(x.shape[2], dtype=jnp.float32)
    ang = pos[:, None] * inv_freq[None, :]
    ang = jnp.concatenate([ang, ang], axis=-1)
    cos, sin = jnp.cos(ang), jnp.sin(ang)
    xr = x[..., :rot].astype(jnp.float32)
    x1, x2 = xr[..., : rot // 2], xr[..., rot // 2:]
    rotated = xr * cos + jnp.concatenate([-x2, x1], axis=-1) * sin
    return jnp.concatenate([rotated.astype(x.dtype), x[..., rot:]], axis=-1)


def moba_attention(q, k, v):
    b, h, t, d = q.shape
    nb = -(-t // MOBA_BLOCK)
    tp = nb * MOBA_BLOCK
    if tp != t:
        pad = ((0, 0), (0, 0), (0, tp - t), (0, 0))
        q, k, v = jnp.pad(q, pad), jnp.pad(k, pad), jnp.pad(v, pad)
    scale = d ** -0.5
    k_blocks = k.reshape(b, h, nb, MOBA_BLOCK, d)
    v_blocks = v.reshape(b, h, nb, MOBA_BLOCK, d)
    k_mean = jnp.mean(k_blocks.astype(jnp.float32), axis=3)
    gate = jnp.einsum('bhtd,bhnd->bhtn', q.astype(jnp.float32), k_mean)
    q_block = jnp.arange(tp) // MOBA_BLOCK
    fully_past = jnp.arange(nb)[None, :] < q_block[:, None]
    gate = jnp.where(fully_past, gate, -jnp.inf)
    n_sel = min(MOBA_TOPK, nb)
    _, top_idx = lax.top_k(gate, n_sel)
    bi = jnp.arange(b)[:, None, None, None]
    hi = jnp.arange(h)[None, :, None, None]
    n_chunks = tp // MOBA_Q_CHUNK

    def one_chunk(c):
        start = c * MOBA_Q_CHUNK
        own = start // MOBA_BLOCK
        q_c = lax.dynamic_slice_in_dim(q, start, MOBA_Q_CHUNK, axis=2).astype(jnp.float32) * scale
        idx_c = lax.dynamic_slice_in_dim(top_idx, start, MOBA_Q_CHUNK, axis=2)
        k_sel = k_blocks[bi, hi, idx_c]
        v_sel = v_blocks[bi, hi, idx_c]
        k_own = lax.dynamic_slice_in_dim(k, own * MOBA_BLOCK, MOBA_BLOCK, axis=2)
        v_own = lax.dynamic_slice_in_dim(v, own * MOBA_BLOCK, MOBA_BLOCK, axis=2)
        s_sel = jnp.einsum('bhqd,bhqnsd->bhqns', q_c, k_sel)
        s_sel = jnp.where((idx_c < own)[..., None], s_sel, -jnp.inf)
        s_own = jnp.einsum('bhqd,bhsd->bhqs', q_c, k_own)
        q_pos = start + jnp.arange(MOBA_Q_CHUNK)
        k_pos = own * MOBA_BLOCK + jnp.arange(MOBA_BLOCK)
        s_own = jnp.where(k_pos[None, :] <= q_pos[:, None], s_own, -jnp.inf)
        scores = jnp.concatenate([s_sel.reshape(b, h, MOBA_Q_CHUNK, n_sel * MOBA_BLOCK), s_own], axis=-1)
        p = jax.nn.softmax(scores, axis=-1)
        p_sel = p[..., : n_sel * MOBA_BLOCK].reshape(b, h, MOBA_Q_CHUNK, n_sel, MOBA_BLOCK)
        p_own = p[..., n_sel * MOBA_BLOCK:]
        return (jnp.einsum('bhqns,bhqnsd->bhqd', p_sel, v_sel)
                + jnp.einsum('bhqs,bhsd->bhqd', p_own, v_own))

    o = lax.map(one_chunk, jnp.arange(n_chunks))
    o = o.transpose(1, 2, 0, 3, 4).reshape(b, h, tp, d)[:, :, :t]
    return o.astype(v.dtype)


def retention_chunked(q, k, v, log_gamma):
    b, h, t, dk = q.shape
    dv = v.shape[-1]
    c = RET_CHUNK
    n = t // c
    k = k * dk ** -0.5
    q, k, v = (a.reshape(b, h, n, c, a.shape[-1]) for a in (q, k, v))
    i = jnp.arange(c, dtype=jnp.float32)
    lg = log_gamma[:, None]
    tril = jnp.tril(jnp.ones((c, c), dtype=bool))
    rel = i[:, None] - i[None, :]
    decay = jnp.where(tril, jnp.exp(jnp.where(tril, lg[:, :, None] * rel, 0.0)), 0.0)
    scores = jnp.einsum('bhncd,bhnsd->bhncs', q, k) * decay[None, :, None]
    inner = jnp.einsum('bhncs,bhnse->bhnce', scores, v)
    zeta = jnp.exp(lg * (c - 1 - i))
    chunk_state = jnp.einsum('bhnsd,bhnse->bhnde', k * zeta[None, :, None, :, None], v)
    gamma_c = jnp.exp(log_gamma * c)[None, :, None, None]

    def step(state, u):
        return state * gamma_c + u, state

    _, prev = lax.scan(step, jnp.zeros((b, h, dk, dv), jnp.float32), jnp.moveaxis(chunk_state, 2, 0))
    prev = jnp.moveaxis(prev, 0, 2)
    xi = jnp.exp(lg * (i + 1.0))
    cross = jnp.einsum('bhncd,bhnde->bhnce', q, prev) * xi[None, :, None, :, None]
    return (inner + cross).reshape(b, h, t, dv)


def causal_depthwise_conv(x, w):
    kw, ch = w.shape
    return lax.conv_general_dilated(x, w[:, None, :], window_strides=(1,), padding=[(kw - 1, 0)],
                                    dimension_numbers=('NWC', 'WIO', 'NWC'), feature_group_count=ch)


def gated_delta_rule_chunked(q, k, v, beta, g):
    b, h, t, dk = q.shape
    dv = v.shape[-1]
    c = GDN_CHUNK
    n = t // c
    q = q * dk ** -0.5
    q, k, v = (a.reshape(b, h, n, c, a.shape[-1]) for a in (q, k, v))
    beta = beta.reshape(b, h, n, c)
    gcum = jnp.cumsum(g.reshape(b, h, n, c), axis=-1)
    tril = jnp.tril(jnp.ones((c, c), dtype=bool))
    strict = jnp.tril(jnp.ones((c, c), dtype=bool), -1)
    diff = gcum[..., :, None] - gcum[..., None, :]
    lmask = jnp.where(tril, jnp.exp(jnp.where(tril, diff, 0.0)), 0.0)
    k_beta = k * beta[..., None]
    v_beta = v * beta[..., None]
    a_mat = jnp.where(strict, jnp.einsum('bhncd,bhnsd->bhncs', k_beta, k) * lmask, 0.0)
    rhs = jnp.concatenate([v_beta, k_beta * jnp.exp(gcum)[..., None]], axis=-1)
    sol = lax.linalg.triangular_solve(a_mat + jnp.eye(c, dtype=a_mat.dtype), rhs,
                                      left_side=True, lower=True, unit_diagonal=True)
    u, w = sol[..., :dv], sol[..., dv:]
    attn_intra = jnp.einsum('bhncd,bhnsd->bhncs', q, k) * lmask
    q_dec = q * jnp.exp(gcum)[..., None]
    g_last = gcum[..., -1]
    k_dec = k * jnp.exp(g_last[..., None] - gcum)[..., None]

    def step(state, xs):
        q_n, w_n, u_n, attn_n, k_n, gl_n = xs
        v_new = u_n - jnp.einsum('bhcd,bhde->bhce', w_n, state)
        o_n = jnp.einsum('bhcd,bhde->bhce', q_n, state) + jnp.einsum('bhcs,bhse->bhce', attn_n, v_new)
        state = state * jnp.exp(gl_n)[..., None, None] + jnp.einsum('bhcd,bhce->bhde', k_n, v_new)
        return state, o_n

    xs = tuple(jnp.moveaxis(a, 2, 0) for a in (q_dec, w, u, attn_intra, k_dec, g_last))
    _, o = lax.scan(step, jnp.zeros((b, h, dk, dv), jnp.float32), xs)
    return jnp.moveaxis(o, 0, 2).reshape(b, h, t, dv)


def setup_inputs(seed: int = 0) -> dict:
    key = jax.random.key(seed)
    ks = jax.random.split(key, 20)
    f32 = jnp.float32
    nrm = lambda k_, shape, s: jax.random.normal(k_, shape, f32) * s
    dt = jnp.exp(jax.random.uniform(ks[9], (DEPTH, C_HEADS), f32, minval=np.log(1e-3), maxval=np.log(1e-1)))
    return {
        'x': jax.random.normal(ks[0], (BATCH, SEQ, D_MODEL), f32),
        'attn_norm': 1.0 + nrm(ks[1], (DEPTH, D_MODEL), 0.02),
        'w_in': nrm(ks[2], (DEPTH, D_MODEL, IN_COLS), D_MODEL ** -0.5),
        'q_norm': 1.0 + nrm(ks[3], (DEPTH, HEAD_DIM), 0.02),
        'k_norm': 1.0 + nrm(ks[4], (DEPTH, HEAD_DIM), 0.02),
        'ret_norm': 1.0 + nrm(ks[5], (DEPTH, B_WIDTH), 0.02),
        'conv_w': nrm(ks[6], (DEPTH, CONV_WIDTH, 3 * C_WIDTH), CONV_WIDTH ** -0.5),
        'a_log': jnp.log(jax.random.uniform(ks[7], (DEPTH, C_HEADS), f32, minval=1.0, maxval=16.0)),
        'dt_bias': dt + jnp.log(-jnp.expm1(-dt)),
        'gdn_norm': 1.0 + nrm(ks[8], (DEPTH, HEAD_DIM), 0.02),
        'w_branch': nrm(ks[10], (DEPTH, MIX_WIDTH, D_MODEL), C_WIDTH ** -0.5),
        'w_out': nrm(ks[11], (DEPTH, D_MODEL, D_MODEL), D_MODEL ** -0.5),
        'ffn_norm': 1.0 + nrm(ks[12], (DEPTH, D_MODEL), 0.02),
        'w_gate': nrm(ks[13], (DEPTH, D_MODEL, FFN_HIDDEN), D_MODEL ** -0.5),
        'w_up': nrm(ks[14], (DEPTH, D_MODEL, FFN_HIDDEN), D_MODEL ** -0.5),
        'w_down': nrm(ks[15], (DEPTH, FFN_HIDDEN, D_MODEL), FFN_HIDDEN ** -0.5),
    }


def reference(x, attn_norm, w_in, q_norm, k_norm, ret_norm, conv_w, a_log, dt_bias, gdn_norm,
              w_branch, w_out, ffn_norm, w_gate, w_up, w_down):
    f32 = jnp.float32
    b, t, _ = x.shape
    split_points = [int(s) for s in np.cumsum(SPLIT_SIZES)[:-1]]
    rope_freq = ROPE_THETA ** (-jnp.arange(0, ROPE_DIM, 2, dtype=f32) / ROPE_DIM)
    ret_freq = RET_THETA ** (-jnp.linspace(0.0, 1.0, HEAD_DIM // 2, dtype=f32))
    ret_log_gamma = jnp.log1p(-jnp.exp2(-5.0 - jnp.arange(B_HEADS, dtype=f32)))

    for layer in range(DEPTH):
        h = rmsnorm(x, attn_norm[layer])
        proj = h @ w_in[layer]
        (qa, ka, va, qb, kb, vb, gb, qc, kc, vc, zc, beta_logit, alpha_logit,
         gate_logit) = jnp.split(proj, split_points, axis=-1)

        qa_h = apply_rotary(rmsnorm(split_heads(qa, A_HEADS), q_norm[layer]), rope_freq)
        ka_h = apply_rotary(rmsnorm(split_heads(ka, A_HEADS), k_norm[layer]), rope_freq)
        o_a = merge_heads(moba_attention(qa_h, ka_h, split_heads(va, A_HEADS)))

        qb_h = apply_rotary(split_heads(qb, B_HEADS), ret_freq).astype(f32)
        kb_h = apply_rotary(split_heads(kb, B_HEADS), ret_freq).astype(f32)
        ret = retention_chunked(qb_h, kb_h, split_heads(vb, B_HEADS).astype(f32), ret_log_gamma)
        ret = rmsnorm(ret, ret_norm[layer].reshape(B_HEADS, 1, HEAD_DIM).astype(f32))
        o_b = merge_heads(ret).astype(x.dtype) * jax.nn.silu(gb)

        qkv_c = jax.nn.silu(causal_depthwise_conv(jnp.concatenate([qc, kc, vc], axis=-1), conv_w[layer]))
        qc2, kc2, vc2 = jnp.split(qkv_c, 3, axis=-1)
        qc_h = l2norm(split_heads(qc2, C_HEADS).astype(f32))
        kc_h = l2norm(split_heads(kc2, C_HEADS).astype(f32))
        vc_h = split_heads(vc2, C_HEADS).astype(f32)
        beta = jax.nn.sigmoid(beta_logit.astype(f32)).transpose(0, 2, 1)
        g = (-jnp.exp(a_log[layer].astype(f32))
             * jax.nn.softplus(alpha_logit.astype(f32) + dt_bias[layer].astype(f32))).transpose(0, 2, 1)
        o_gdn = gated_delta_rule_chunked(qc_h, kc_h, vc_h, beta, g)
        o_gdn = rmsnorm(o_gdn, gdn_norm[layer].astype(f32))
        o_c = merge_heads(o_gdn).astype(x.dtype) * jax.nn.silu(zc)

        gates = jax.nn.sigmoid(gate_logit).reshape(b, t, N_BRANCHES, D_MODEL)
        wb = w_branch[layer]
        merged = (gates[:, :, 0] * (o_a @ wb[:A_WIDTH])
                  + gates[:, :, 1] * (o_b @ wb[A_WIDTH:A_WIDTH + B_WIDTH])
                  + gates[:, :, 2] * (o_c @ wb[A_WIDTH + B_WIDTH:]))
        x = x + merged @ w_out[layer]

        h2 = rmsnorm(x, ffn_norm[layer])
        x = x + (jax.nn.silu(h2 @ w_gate[layer]) * (h2 @ w_up[layer])) @ w_down[layer]
    return x
```

```python
import functools

import jax
import jax.numpy as jnp
import numpy as np
from jax import lax
from jax.experimental import pallas as pl
from jax.experimental.pallas import tpu as pltpu

F32 = jnp.float32
BF16 = jnp.bfloat16

D_MODEL = 2048
HEAD_DIM = 128
A_HEADS = 6
B_HEADS = 5
C_HEADS = 5
A_WIDTH = A_HEADS * HEAD_DIM
B_WIDTH = B_HEADS * HEAD_DIM
C_WIDTH = C_HEADS * HEAD_DIM
N_BRANCHES = 3
MOBA_BLOCK = 256
MOBA_TOPK = 3
ROPE_THETA = 500000.0
ROPE_DIM = HEAD_DIM // 4
RET_THETA = 10000.0
RET_CHUNK = 128
GDN_CHUNK = 64
CONV_WIDTH = 4
FFN_HIDDEN = 5632
NORM_EPS = 1e-6

MAIN_COLS = 3 * A_WIDTH + 4 * B_WIDTH + 4 * C_WIDTH
BLK_QA, BLK_KA, BLK_VA = 0, 6, 12
BLK_QB, BLK_KB, BLK_VB, BLK_GB = 18, 23, 28, 33
BLK_QC, BLK_KC, BLK_VC, BLK_ZC = 38, 43, 48, 53
BLK_BA = 58
GATE_COL0 = 7680
PROJ_COLS = GATE_COL0 + N_BRANCHES * D_MODEL

GDN_GROUP = 4 * GDN_CHUNK
NEG_BIG = -1e30
VMEM_LIMIT = 52 * 1024 * 1024


def _cparams(sem):
    return pltpu.CompilerParams(dimension_semantics=sem, vmem_limit_bytes=VMEM_LIMIT)


def _sigmoid(x):
    return 1.0 / (1.0 + jnp.exp(-x))


def _silu(x):
    return x * _sigmoid(x)


def _dot(a, b, precision=None):
    return jnp.dot(a, b, preferred_element_type=F32, precision=precision)


def _dot_nt(a, b, precision=None):
    return lax.dot_general(a, b, (((1,), (1,)), ((), ())), preferred_element_type=F32, precision=precision)


def _norm_rows_to(h_ref, x_ref, g_ref, rows_per_step=256):
    n = x_ref.shape[0] // rows_per_step

    def body(r, c):
        rows = pl.ds(pl.multiple_of(r * rows_per_step, rows_per_step), rows_per_step)
        x = x_ref[rows, :]
        ms = jnp.mean(x * x, axis=-1, keepdims=True)
        h_ref[rows, :] = (x * lax.rsqrt(ms + NORM_EPS) * g_ref[...]).astype(BF16)
        return c

    lax.fori_loop(0, n, body, 0)


def _in_proj_kernel(x_ref, g_ref, w_ref, o_ref, h_ref):
    @pl.when(pl.program_id(1) == 0)
    def _():
        _norm_rows_to(h_ref, x_ref, g_ref)

    o_ref[...] = _dot(h_ref[...], w_ref[...])


def in_proj(x, gain, w, tm=1024, tn=512):
    n, d = x.shape
    cols = w.shape[1]
    return pl.pallas_call(
        _in_proj_kernel,
        grid=(n // tm, cols // tn),
        in_specs=[
            pl.BlockSpec((tm, d), lambda i, j: (i, 0)),
            pl.BlockSpec((1, d), lambda i, j: (0, 0)),
            pl.BlockSpec((d, tn), lambda i, j: (0, j)),
        ],
        out_specs=pl.BlockSpec((tm, tn), lambda i, j: (i, j)),
        out_shape=jax.ShapeDtypeStruct((n, cols), F32),
        scratch_shapes=[pltpu.VMEM((tm, d), BF16)],
        compiler_params=_cparams(("parallel", "arbitrary")),
        name="in_proj",
    )(x, gain, w)


def _ffn_up_kernel(x_ref, g_ref, wg_ref, wu_ref, o_ref, h_ref):
    @pl.when(pl.program_id(1) == 0)
    def _():
        _norm_rows_to(h_ref, x_ref, g_ref)

    h = h_ref[...]
    o_ref[...] = (_silu(_dot(h, wg_ref[...])) * _dot(h, wu_ref[...])).astype(BF16)


def ffn_up(x, gain, wg, wu, tm=1024, tn=512):
    n, d = x.shape
    cols = wg.shape[1]
    return pl.pallas_call(
        _ffn_up_kernel,
        grid=(n // tm, cols // tn),
        in_specs=[
            pl.BlockSpec((tm, d), lambda i, j: (i, 0)),
            pl.BlockSpec((1, d), lambda i, j: (0, 0)),
            pl.BlockSpec((d, tn), lambda i, j: (0, j)),
            pl.BlockSpec((d, tn), lambda i, j: (0, j)),
        ],
        out_specs=pl.BlockSpec((tm, tn), lambda i, j: (i, j)),
        out_shape=jax.ShapeDtypeStruct((n, cols), BF16),
        scratch_shapes=[pltpu.VMEM((tm, d), BF16)],
        compiler_params=_cparams(("parallel", "arbitrary")),
        name="ffn_up",
    )(x, gain, wg, wu)


def _residual_matmul_kernel(a_ref, w_ref, x_ref, o_ref):
    o_ref[...] = x_ref[...] + _dot(a_ref[...], w_ref[...])


def residual_matmul(a, w, x, tm, tn, name):
    n, k = a.shape
    d = w.shape[1]
    return pl.pallas_call(
        _residual_matmul_kernel,
        grid=(n // tm, d // tn),
        in_specs=[
            pl.BlockSpec((tm, k), lambda i, j: (i, 0)),
            pl.BlockSpec((k, tn), lambda i, j: (0, j)),
            pl.BlockSpec((tm, tn), lambda i, j: (i, j)),
        ],
        out_specs=pl.BlockSpec((tm, tn), lambda i, j: (i, j)),
        out_shape=jax.ShapeDtypeStruct((n, d), F32),
        compiler_params=_cparams(("parallel", "parallel")),
        name=name,
    )(a, w, x)


def _merge_kernel(oa_ref, ob_ref, oc_ref, wa_ref, wb_ref, wc_ref, ga_ref, gb_ref, gc_ref, o_ref):
    merged = _sigmoid(ga_ref[...]) * _dot(oa_ref[...], wa_ref[...])
    merged += _sigmoid(gb_ref[...]) * _dot(ob_ref[...], wb_ref[...])
    merged += _sigmoid(gc_ref[...]) * _dot(oc_ref[...], wc_ref[...])
    o_ref[...] = merged.astype(BF16)


def merge_branches(o_a, o_b, o_c, wa, wb, wc, proj, tm=1024, tn=512):
    n = o_a.shape[0]
    gate_blk0 = GATE_COL0 // tn
    per_branch = D_MODEL // tn

    def gate_spec(br):
        return pl.BlockSpec((tm, tn), lambda i, j: (i, gate_blk0 + br * per_branch + j))

    return pl.pallas_call(
        _merge_kernel,
        grid=(n // tm, D_MODEL // tn),
        in_specs=[
            pl.BlockSpec((tm, A_WIDTH), lambda i, j: (i, 0)),
            pl.BlockSpec((tm, B_WIDTH), lambda i, j: (i, 0)),
            pl.BlockSpec((tm, C_WIDTH), lambda i, j: (i, 0)),
            pl.BlockSpec((A_WIDTH, tn), lambda i, j: (0, j)),
            pl.BlockSpec((B_WIDTH, tn), lambda i, j: (0, j)),
            pl.BlockSpec((C_WIDTH, tn), lambda i, j: (0, j)),
            gate_spec(0), gate_spec(1), gate_spec(2),
        ],
        out_specs=pl.BlockSpec((tm, tn), lambda i, j: (i, j)),
        out_shape=jax.ShapeDtypeStruct((n, D_MODEL), BF16),
        compiler_params=_cparams(("parallel", "parallel")),
        name="merge_branches",
    )(o_a, o_b, o_c, wa, wb, wc, proj, proj, proj)


def _rope_partial(x, cos, sin):
    half = ROPE_DIM // 2
    lane = lax.broadcasted_iota(jnp.int32, x.shape, 1)
    partner = jnp.where(lane < half, pltpu.roll(x, HEAD_DIM - half, 1), pltpu.roll(x, half, 1))
    return x * cos + partner * sin


def _rope_full(x, cos, sin):
    return x * cos + pltpu.roll(x, HEAD_DIM // 2, 1) * sin


def _rotary_tables(t, inv_freq, rot):
    pos = jnp.arange(t, dtype=F32)
    ang = pos[:, None] * inv_freq[None, :]
    ang = jnp.concatenate([ang, ang], axis=-1)
    cos, sin = jnp.cos(ang), jnp.sin(ang)
    sign = jnp.concatenate([-jnp.ones((rot // 2,), F32), jnp.ones((rot // 2,), F32)])
    pad = HEAD_DIM - rot
    cos = jnp.concatenate([cos, jnp.ones((t, pad), F32)], axis=-1)
    sin = jnp.concatenate([sin * sign[None, :], jnp.zeros((t, pad), F32)], axis=-1)
    return cos, sin


def _moba_kprep_kernel(k_ref, v_ref, g_ref, cos_ref, sin_ref, k3_ref, vt3_ref, kmean_ref):
    nb = k3_ref.shape[0]
    for j in range(nb):
        rows = slice(j * MOBA_BLOCK, (j + 1) * MOBA_BLOCK)
        x = k_ref[rows, :]
        ms = jnp.mean(x * x, axis=-1, keepdims=True)
        kn = x * lax.rsqrt(ms + NORM_EPS) * g_ref[...]
        kr = _rope_partial(kn, cos_ref[rows, :], sin_ref[rows, :])
        k3_ref[j] = kr.astype(BF16)
        kmean_ref[0, j:j + 1, :] = jnp.mean(kr, axis=0, keepdims=True)
        vt3_ref[j] = v_ref[rows, :].T.astype(BF16)


def moba_kprep(proj, k_gain, cos, sin, b, t):
    nb = t // MOBA_BLOCK
    bh = b * A_HEADS
    return pl.pallas_call(
        _moba_kprep_kernel,
        grid=(b, A_HEADS),
        in_specs=[
            pl.BlockSpec((t, HEAD_DIM), lambda i, h: (i, BLK_KA + h)),
            pl.BlockSpec((t, HEAD_DIM), lambda i, h: (i, BLK_VA + h)),
            pl.BlockSpec((1, HEAD_DIM), lambda i, h: (0, 0)),
            pl.BlockSpec((t, HEAD_DIM), lambda i, h: (0, 0)),
            pl.BlockSpec((t, HEAD_DIM), lambda i, h: (0, 0)),
        ],
        out_specs=[
            pl.BlockSpec((nb, MOBA_BLOCK, HEAD_DIM), lambda i, h: (i * A_HEADS + h, 0, 0)),
            pl.BlockSpec((nb, HEAD_DIM, MOBA_BLOCK), lambda i, h: (i * A_HEADS + h, 0, 0)),
            pl.BlockSpec((1, nb, HEAD_DIM), lambda i, h: (i * A_HEADS + h, 0, 0)),
        ],
        out_shape=[
            jax.ShapeDtypeStruct((bh * nb, MOBA_BLOCK, HEAD_DIM), BF16),
            jax.ShapeDtypeStruct((bh * nb, HEAD_DIM, MOBA_BLOCK), BF16),
            jax.ShapeDtypeStruct((bh, nb, HEAD_DIM), F32),
        ],
        compiler_params=_cparams(("parallel", "parallel")),
        name="moba_kprep",
    )(proj, proj, k_gain, cos, sin)


def _moba_attn_kernel(q_ref, g_ref, cos_ref, sin_ref, k3_ref, vt3_ref, kmean_ref, o_ref, bias_ref):
    nb = k3_ref.shape[0]
    qi = pl.program_id(2)
    x = q_ref[...]
    ms = jnp.mean(x * x, axis=-1, keepdims=True)
    qn = x * lax.rsqrt(ms + NORM_EPS) * g_ref[...]
    qr_t = _rope_partial(qn, cos_ref[...], sin_ref[...]).T

    gate = _dot(kmean_ref[0], qr_t, precision=lax.Precision.HIGHEST)
    blk = lax.broadcasted_iota(jnp.int32, gate.shape, 0)
    past = blk < qi
    gate = jnp.where(past, gate, -jnp.inf)
    rank = jnp.zeros(gate.shape, F32)
    for i in range(nb):
        gi = gate[i:i + 1, :]
        wins = jnp.where(gi > gate, 1.0, jnp.where(gi == gate, jnp.where(blk > i, 1.0, 0.0), 0.0))
        rank = rank + wins
    bias = jnp.where(past, jnp.where(rank < MOBA_TOPK, 0.0, NEG_BIG), NEG_BIG)
    for i in range(nb):
        bias_ref[i] = jnp.broadcast_to(bias[i:i + 1, :], bias_ref.shape[1:])

    qs_t = (qr_t * (HEAD_DIM ** -0.5)).astype(BF16)

    s = _dot(k3_ref[qi], qs_t)
    key_pos = lax.broadcasted_iota(jnp.int32, s.shape, 0)
    q_pos = lax.broadcasted_iota(jnp.int32, s.shape, 1)
    s = jnp.where(key_pos <= q_pos, s, NEG_BIG)
    m0 = jnp.max(s, axis=0, keepdims=True)
    p = jnp.exp(s - m0)
    l0 = jnp.sum(p, axis=0, keepdims=True)
    acc0 = _dot(vt3_ref[qi], p.astype(BF16))

    def body(j, carry):
        m, l, acc = carry
        s = _dot(k3_ref[j], qs_t) + bias_ref[j][0:1, :]
        m_new = jnp.maximum(m, jnp.max(s, axis=0, keepdims=True))
        alpha = jnp.exp(m - m_new)
        p = jnp.exp(s - m_new)
        l = alpha * l + jnp.sum(p, axis=0, keepdims=True)
        acc = alpha * acc + _dot(vt3_ref[j], p.astype(BF16))
        return m_new, l, acc

    _, l, acc = lax.fori_loop(0, qi, body, (m0, l0, acc0))
    o_ref[...] = (acc / l).T.astype(BF16)


def moba_attention(proj, q_gain, cos, sin, k3, vt3, kmean, b, t):
    nb = t // MOBA_BLOCK
    n = b * t
    return pl.pallas_call(
        _moba_attn_kernel,
        grid=(b, A_HEADS, nb),
        in_specs=[
            pl.BlockSpec((MOBA_BLOCK, HEAD_DIM), lambda i, h, q: (i * nb + q, BLK_QA + h)),
            pl.BlockSpec((1, HEAD_DIM), lambda i, h, q: (0, 0)),
            pl.BlockSpec((MOBA_BLOCK, HEAD_DIM), lambda i, h, q: (q, 0)),
            pl.BlockSpec((MOBA_BLOCK, HEAD_DIM), lambda i, h, q: (q, 0)),
            pl.BlockSpec((nb, MOBA_BLOCK, HEAD_DIM), lambda i, h, q: (i * A_HEADS + h, 0, 0)),
            pl.BlockSpec((nb, HEAD_DIM, MOBA_BLOCK), lambda i, h, q: (i * A_HEADS + h, 0, 0)),
            pl.BlockSpec((1, nb, HEAD_DIM), lambda i, h, q: (i * A_HEADS + h, 0, 0)),
        ],
        out_specs=pl.BlockSpec((MOBA_BLOCK, HEAD_DIM), lambda i, h, q: (i * nb + q, h)),
        out_shape=jax.ShapeDtypeStruct((n, A_WIDTH), BF16),
        scratch_shapes=[pltpu.VMEM((nb, 8, MOBA_BLOCK), F32)],
        compiler_params=_cparams(("parallel", "parallel", "arbitrary")),
        name="moba_attention",
    )(proj, q_gain, cos, sin, k3, vt3, kmean)


def _retention_kernel(q_ref, k_ref, v_ref, gate_ref, cos_ref, sin_ref, decay_ref, zeta_ref, xi_ref,
                      gamma_ref, gain_ref, o_ref, state_ref):
    @pl.when(pl.program_id(2) == 0)
    def _():
        state_ref[...] = jnp.zeros_like(state_ref)

    c = RET_CHUNK
    decay = decay_ref[0]
    zeta = zeta_ref[0]
    xi = xi_ref[0]
    gamma_c = gamma_ref[0, 0:1, :]
    state = state_ref[...]
    for n in range(q_ref.shape[0] // c):
        rows = slice(n * c, (n + 1) * c)
        cos, sin = cos_ref[rows, :], sin_ref[rows, :]
        q = _rope_full(q_ref[rows, :], cos, sin)
        k = _rope_full(k_ref[rows, :], cos, sin) * (HEAD_DIM ** -0.5)
        qb, kb, vb = q.astype(BF16), k.astype(BF16), v_ref[rows, :].astype(BF16)
        scores = _dot_nt(qb, kb) * decay
        inner = _dot(scores.astype(BF16), vb)
        cross = _dot(qb, state.astype(BF16)) * xi
        ret = inner + cross
        state = state * gamma_c + _dot((k * zeta).T.astype(BF16), vb)
        ms = jnp.mean(ret * ret, axis=-1, keepdims=True)
        y = ret * lax.rsqrt(ms + NORM_EPS) * gain_ref[0]
        o_ref[rows, :] = (y * _silu(gate_ref[rows, :])).astype(BF16)
    state_ref[...] = state


def retention(proj, cos, sin, decay, zeta, xi, gamma, gain, b, t, rows=512):
    n = b * t
    steps = t // rows

    def col(blk0):
        return pl.BlockSpec((rows, HEAD_DIM), lambda i, h, s: (i * steps + s, blk0 + h))

    def per_head(shape):
        return pl.BlockSpec((1,) + shape, lambda i, h, s: (h, 0, 0))

    return pl.pallas_call(
        _retention_kernel,
        grid=(b, B_HEADS, steps),
        in_specs=[
            col(BLK_QB), col(BLK_KB), col(BLK_VB), col(BLK_GB),
            pl.BlockSpec((rows, HEAD_DIM), lambda i, h, s: (s, 0)),
            pl.BlockSpec((rows, HEAD_DIM), lambda i, h, s: (s, 0)),
            per_head((RET_CHUNK, RET_CHUNK)), per_head((RET_CHUNK, HEAD_DIM)), per_head((RET_CHUNK, HEAD_DIM)),
            per_head((8, HEAD_DIM)), per_head((1, HEAD_DIM)),
        ],
        out_specs=pl.BlockSpec((rows, HEAD_DIM), lambda i, h, s: (i * steps + s, h)),
        out_shape=jax.ShapeDtypeStruct((n, B_WIDTH), BF16),
        scratch_shapes=[pltpu.VMEM((HEAD_DIM, HEAD_DIM), F32)],
        compiler_params=_cparams(("parallel", "parallel", "arbitrary")),
        name="retention",
    )(proj, proj, proj, proj, cos, sin, decay, zeta, xi, gamma, gain)


def _retention_tables():
    c = RET_CHUNK
    log_gamma = jnp.log1p(-jnp.exp2(-5.0 - jnp.arange(B_HEADS, dtype=F32)))
    i = jnp.arange(c, dtype=F32)
    lg = log_gamma[:, None]
    tril = jnp.tril(jnp.ones((c, c), dtype=bool))
    rel = i[:, None] - i[None, :]
    decay = jnp.where(tril, jnp.exp(jnp.where(tril, lg[:, :, None] * rel, 0.0)), 0.0)
    zeta = jnp.exp(lg * (c - 1 - i))
    xi = jnp.exp(lg * (i + 1.0))
    gamma_c = jnp.exp(log_gamma * c)
    bcast = lambda v: jnp.broadcast_to(v[:, :, None], (B_HEADS, c, HEAD_DIM))
    gamma = jnp.broadcast_to(gamma_c[:, None, None], (B_HEADS, 8, HEAD_DIM))
    return decay, bcast(zeta), bcast(xi), gamma


def _lane_column(x, idx):
    lane = lax.broadcasted_iota(jnp.int32, x.shape, 1)
    col = jnp.sum(jnp.where(lane == idx, x, 0.0), axis=-1, keepdims=True)
    return jnp.broadcast_to(col, x.shape)


def _gdn_kernel(q_ref, k_ref, v_ref, z_ref, ba_ref, cwq_ref, cwk_ref, cwv_ref, alog_ref, dtb_ref, gain_ref,
                o_ref, state_ref):
    r = GDN_GROUP
    c = GDN_CHUNK
    d = HEAD_DIM
    h = pl.program_id(1)
    hi = lax.Precision.HIGHEST
    state_ref[...] = jnp.zeros_like(state_ref)

    row = lax.broadcasted_iota(jnp.int32, (r, r), 0)
    col = lax.broadcasted_iota(jnp.int32, (r, r), 1)
    same_chunk = (row // c) == (col // c)
    tril = same_chunk & (row >= col)
    strict = same_chunk & (row > col)
    cum_mat = jnp.where(tril, 1.0, 0.0)
    tot_mat = jnp.where(same_chunk, 1.0, 0.0)
    lane_r = lax.broadcasted_iota(jnp.int32, (d, r), 1)

    def conv_silu(x_ref, w_ref, r0, first):
        prev = x_ref[pl.ds(pl.multiple_of(jnp.maximum(r0 - 8, 0), 8), 8), :]
        prev = jnp.where(first, 0.0, prev)
        xe = jnp.concatenate([prev, x_ref[pl.ds(r0, r), :]], axis=0)
        y = xe[8:, :] * w_ref[CONV_WIDTH - 1:CONV_WIDTH, :]
        for back in range(1, CONV_WIDTH):
            tap = CONV_WIDTH - 1 - back
            y = y + pltpu.roll(xe, back, 0)[8:, :] * w_ref[tap:tap + 1, :]
        return _silu(y)

    def group(gidx, carry):
        r0 = pl.multiple_of(gidx * r, r)
        first = gidx == 0
        xq = conv_silu(q_ref, cwq_ref, r0, first)
        xk = conv_silu(k_ref, cwk_ref, r0, first)
        v = conv_silu(v_ref, cwv_ref, r0, first)
        q = xq * lax.rsqrt(jnp.sum(xq * xq, axis=-1, keepdims=True) + NORM_EPS) * (d ** -0.5)
        k = xk * lax.rsqrt(jnp.sum(xk * xk, axis=-1, keepdims=True) + NORM_EPS)

        ba = ba_ref[pl.ds(r0, r), :]
        beta = _lane_column(_sigmoid(ba), h)
        pre = ba + dtb_ref[...]
        softplus = jnp.maximum(pre, 0.0) + jnp.log(1.0 + jnp.exp(-jnp.abs(pre)))
        g = _lane_column(-jnp.exp(alog_ref[...]) * softplus, C_HEADS + h)

        gcum = _dot(cum_mat, g, precision=hi)
        gtot = _dot(tot_mat, g, precision=hi)
        gcum_col = jnp.concatenate([gcum, gcum], axis=1)
        diff = gcum_col - gcum_col.T
        lmask = jnp.where(tril, jnp.exp(jnp.where(tril, diff, 0.0)), 0.0)
        exp_g = jnp.exp(gcum)

        k_beta = k * beta
        kb16 = k.astype(BF16)
        a_mat = jnp.where(strict, _dot_nt(k_beta.astype(BF16), kb16) * lmask, 0.0)
        attn = (_dot_nt(q.astype(BF16), kb16) * lmask).astype(BF16)

        x = jnp.concatenate([v * beta, k_beta * exp_g], axis=1)
        p = -a_mat
        steps = int(np.log2(c))
        for s in range(steps):
            x = x + _dot(p, x, precision=hi)
            if s + 1 < steps:
                p = _dot(p, p, precision=hi)
        u = x[:, :d]
        w16 = x[:, d:].astype(BF16)
        qdec16 = (q * exp_g).astype(BF16)
        kdec_t = (k * jnp.exp(gtot - gcum)).T
        chunk_decay = jnp.exp(gtot)

        state = state_ref[...]
        vnew = []
        for n in range(r // c):
            rows = slice(n * c, (n + 1) * c)
            s16 = state.astype(BF16)
            v_new = u[rows, :] - _dot(w16[rows, :], s16)
            vnew.append(v_new.astype(BF16))
            vfull = jnp.concatenate(vnew + [jnp.zeros((c, d), BF16)] * (r // c - 1 - n), axis=0)
            o = _dot(qdec16[rows, :], s16) + _dot(attn[rows, :], vfull)
            kd = jnp.where((lane_r // c) == n, kdec_t, 0.0).astype(BF16)
            state = state * chunk_decay[n * c:n * c + 1, :] + _dot(kd, vfull)
            ms = jnp.mean(o * o, axis=-1, keepdims=True)
            y = o * lax.rsqrt(ms + NORM_EPS) * gain_ref[...]
            zrows = pl.ds(r0 + n * c, c)
            o_ref[zrows, :] = (y * _silu(z_ref[zrows, :])).astype(BF16)
        state_ref[...] = state
        return carry

    lax.fori_loop(0, q_ref.shape[0] // r, group, 0)


def gated_deltanet(proj, conv_w, alog, dtb, gain, b, t):
    n = b * t

    def col(blk0):
        return pl.BlockSpec((t, HEAD_DIM), lambda i, h: (i, blk0 + h))

    def conv(off):
        return pl.BlockSpec((CONV_WIDTH, HEAD_DIM), lambda i, h: (0, off + h))

    vec = pl.BlockSpec((1, HEAD_DIM), lambda i, h: (0, 0))
    return pl.pallas_call(
        _gdn_kernel,
        grid=(b, C_HEADS),
        in_specs=[
            col(BLK_QC), col(BLK_KC), col(BLK_VC), col(BLK_ZC),
            pl.BlockSpec((t, HEAD_DIM), lambda i, h: (i, BLK_BA)),
            conv(0), conv(C_HEADS), conv(2 * C_HEADS),
            vec, vec, vec,
        ],
        out_specs=pl.BlockSpec((t, HEAD_DIM), lambda i, h: (i, h)),
        out_shape=jax.ShapeDtypeStruct((n, C_WIDTH), BF16),
        scratch_shapes=[pltpu.VMEM((HEAD_DIM, HEAD_DIM), F32)],
        compiler_params=_cparams(("parallel", "parallel")),
        name="gated_deltanet",
    )(proj, proj, proj, proj, proj, conv_w, conv_w, conv_w, alog, dtb, gain)


def _pad_lanes(v, fill=0.0):
    return jnp.concatenate([v, jnp.full((HEAD_DIM - v.shape[0],), fill, F32)])[None, :]


def kernel(x, attn_norm, w_in, q_norm, k_norm, ret_norm, conv_w, a_log, dt_bias, gdn_norm,
           w_branch, w_out, ffn_norm, w_gate, w_up, w_down):
    b, t, d = x.shape
    n = b * t
    depth = w_in.shape[0]
    xf = x.reshape(n, d)

    rope_freq = ROPE_THETA ** (-jnp.arange(0, ROPE_DIM, 2, dtype=F32) / ROPE_DIM)
    ret_freq = RET_THETA ** (-jnp.linspace(0.0, 1.0, HEAD_DIM // 2, dtype=F32))
    cos_a, sin_a = _rotary_tables(t, rope_freq, ROPE_DIM)
    cos_b, sin_b = _rotary_tables(t, ret_freq, HEAD_DIM)
    decay, zeta, xi, gamma = _retention_tables()

    n_ba = 2 * C_HEADS
    ba_cols = MAIN_COLS + n_ba
    for layer in range(depth):
        wl = w_in[layer]
        w_proj = jnp.concatenate(
            [wl[:, :ba_cols], jnp.zeros((d, GATE_COL0 - ba_cols), F32), wl[:, ba_cols:]], axis=1).astype(BF16)
        proj = in_proj(xf, attn_norm[layer][None, :], w_proj)

        k3, vt3, kmean = moba_kprep(proj, k_norm[layer][None, :], cos_a, sin_a, b, t)
        o_a = moba_attention(proj, q_norm[layer][None, :], cos_a, sin_a, k3, vt3, kmean, b, t)

        o_b = retention(proj, cos_b, sin_b, decay, zeta, xi, gamma,
                        ret_norm[layer].reshape(B_HEADS, 1, HEAD_DIM), b, t)

        alog_v = jnp.concatenate([jnp.zeros((C_HEADS,), F32), a_log[layer]])
        dtb_v = jnp.concatenate([jnp.zeros((C_HEADS,), F32), dt_bias[layer]])
        o_c = gated_deltanet(proj, conv_w[layer], _pad_lanes(alog_v), _pad_lanes(dtb_v),
                             gdn_norm[layer][None, :], b, t)

        wbr = w_branch[layer].astype(BF16)
        merged = merge_branches(o_a, o_b, o_c, wbr[:A_WIDTH], wbr[A_WIDTH:A_WIDTH + B_WIDTH],
                                wbr[A_WIDTH + B_WIDTH:], proj)
        xf = residual_matmul(merged, w_out[layer].astype(BF16), xf, 1024, 512, "out_proj")

        hidden = ffn_up(xf, ffn_norm[layer][None, :], w_gate[layer].astype(BF16), w_up[layer].astype(BF16))
        xf = residual_matmul(hidden, w_down[layer].astype(BF16), xf, 512, 512, "ffn_down")
    return xf.reshape(b, t, d)
```

```python
import jax
import jax.numpy as jnp
import numpy as np
from jax import lax
from jax.experimental import pallas as pl
from jax.experimental.pallas import tpu as pltpu

F32 = jnp.float32
BF16 = jnp.bfloat16

D_MODEL = 2048
HEAD_DIM = 128
A_HEADS = 6
B_HEADS = 5
C_HEADS = 5
A_WIDTH = A_HEADS * HEAD_DIM
B_WIDTH = B_HEADS * HEAD_DIM
C_WIDTH = C_HEADS * HEAD_DIM
N_BRANCHES = 3
MOBA_BLOCK = 256
MOBA_TOPK = 3
ROPE_THETA = 500000.0
ROPE_DIM = HEAD_DIM // 4
RET_THETA = 10000.0
RET_CHUNK = 128
GDN_CHUNK = 64
CONV_WIDTH = 4
NORM_EPS = 1e-6

BLK_QA, BLK_KA, BLK_VA = 0, 6, 12
BLK_BA = 18
BLK_QC, BLK_KC, BLK_VC, BLK_ZC = 20, 25, 30, 35
BLK_QB, BLK_KB, BLK_VB, BLK_GB = 40, 45, 50, 55
BLK_GATE = 60
N_BA = 2 * C_HEADS
GATE_COL0 = BLK_GATE * HEAD_DIM
PROJ_BLOCKS = BLK_GATE + N_BRANCHES * D_MODEL // HEAD_DIM
SRC_BLK_B, SRC_BLK_C, SRC_BLK_BA = 18, 38, 58

GDN_GROUP = 4 * GDN_CHUNK
NEG_BIG = -1e30
LOG2_E = 1.4426950408889634
VMEM_LIMIT = 52 * 1024 * 1024


def _cparams(sem):
    return pltpu.CompilerParams(dimension_semantics=sem, vmem_limit_bytes=VMEM_LIMIT)


def _sigmoid(x):
    return 1.0 / (1.0 + jnp.exp(-x))


def _silu(x):
    return x * _sigmoid(x)


def _dot(a, b, precision=None):
    return jnp.dot(a, b, preferred_element_type=F32, precision=precision)


def _dot_nt(a, b, precision=None):
    return lax.dot_general(a, b, (((1,), (1,)), ((), ())), preferred_element_type=F32, precision=precision)


def _w_in_src_block(j, second):
    return jnp.where(
        j < BLK_BA, j,
        jnp.where(j < BLK_QC, SRC_BLK_BA,
                  jnp.where(j < BLK_QB, j - BLK_QC + SRC_BLK_C,
                            jnp.where(j < BLK_GATE, j - BLK_QB + SRC_BLK_B, j - 2 + second))))


def _w_in_prep_kernel(a_ref, b_ref, o_ref):
    j = pl.program_id(1)
    x1 = a_ref[0]
    lane = lax.broadcasted_iota(jnp.int32, x1.shape, 1)

    @pl.when(j >= BLK_GATE)
    def _():
        keep = HEAD_DIM - N_BA
        shifted = jnp.where(lane < keep, pltpu.roll(x1, keep, 1), pltpu.roll(b_ref[0], keep, 1))
        o_ref[0] = shifted.astype(BF16)

    @pl.when(j < BLK_GATE)
    def _():
        ba = jnp.where(lane < N_BA, x1, 0.0)
        out = jnp.where(j == BLK_BA, ba, jnp.where(j == BLK_BA + 1, 0.0, x1))
        o_ref[0] = out.astype(BF16)


def w_in_prep(w_in):
    depth, d, _ = w_in.shape
    return pl.pallas_call(
        _w_in_prep_kernel,
        grid=(depth, PROJ_BLOCKS),
        in_specs=[
            pl.BlockSpec((1, d, HEAD_DIM), lambda l, j: (l, 0, _w_in_src_block(j, 0))),
            pl.BlockSpec((1, d, HEAD_DIM), lambda l, j: (l, 0, _w_in_src_block(j, 1))),
        ],
        out_specs=pl.BlockSpec((1, d, HEAD_DIM), lambda l, j: (l, 0, j)),
        out_shape=jax.ShapeDtypeStruct((depth, d, PROJ_BLOCKS * HEAD_DIM), BF16),
        compiler_params=_cparams(("parallel", "parallel")),
        name="w_in_prep",
    )(w_in, w_in)


def _norm_rows_to(h_ref, x_ref, g_ref, rows_per_step=256):
    n = x_ref.shape[0] // rows_per_step

    def body(r, c):
        rows = pl.ds(pl.multiple_of(r * rows_per_step, rows_per_step), rows_per_step)
        x = x_ref[rows, :]
        ms = jnp.mean(x * x, axis=-1, keepdims=True)
        h_ref[rows, :] = (x * lax.rsqrt(ms + NORM_EPS) * g_ref[...]).astype(BF16)
        return c

    lax.fori_loop(0, n, body, 0)


def _in_proj_kernel(x_ref, g_ref, w_ref, o_ref, h_ref):
    @pl.when(pl.program_id(1) == 0)
    def _():
        _norm_rows_to(h_ref, x_ref, g_ref)

    o_ref[...] = _dot(h_ref[...], w_ref[0])


def in_proj(x, gain, w, layer, tm=1024, tn=512):
    n, d = x.shape
    cols = w.shape[2]
    return pl.pallas_call(
        _in_proj_kernel,
        grid=(n // tm, cols // tn),
        in_specs=[
            pl.BlockSpec((tm, d), lambda i, j: (i, 0)),
            pl.BlockSpec((1, d), lambda i, j: (0, 0)),
            pl.BlockSpec((1, d, tn), lambda i, j: (layer, 0, j)),
        ],
        out_specs=pl.BlockSpec((tm, tn), lambda i, j: (i, j)),
        out_shape=jax.ShapeDtypeStruct((n, cols), F32),
        scratch_shapes=[pltpu.VMEM((tm, d), BF16)],
        compiler_params=_cparams(("parallel", "arbitrary")),
        name="in_proj",
    )(x, gain, w)


def _ffn_up_kernel(x_ref, g_ref, wg_ref, wu_ref, o_ref, h_ref):
    @pl.when(pl.program_id(1) == 0)
    def _():
        _norm_rows_to(h_ref, x_ref, g_ref)

    h = h_ref[...]
    o_ref[...] = (_silu(_dot(h, wg_ref[...])) * _dot(h, wu_ref[...])).astype(BF16)


def ffn_up(x, gain, wg, wu, tm=1024, tn=512):
    n, d = x.shape
    cols = wg.shape[1]
    return pl.pallas_call(
        _ffn_up_kernel,
        grid=(n // tm, cols // tn),
        in_specs=[
            pl.BlockSpec((tm, d), lambda i, j: (i, 0)),
            pl.BlockSpec((1, d), lambda i, j: (0, 0)),
            pl.BlockSpec((d, tn), lambda i, j: (0, j)),
            pl.BlockSpec((d, tn), lambda i, j: (0, j)),
        ],
        out_specs=pl.BlockSpec((tm, tn), lambda i, j: (i, j)),
        out_shape=jax.ShapeDtypeStruct((n, cols), BF16),
        scratch_shapes=[pltpu.VMEM((tm, d), BF16)],
        compiler_params=_cparams(("parallel", "arbitrary")),
        name="ffn_up",
    )(x, gain, wg, wu)


def _residual_matmul_kernel(a_ref, w_ref, x_ref, o_ref):
    o_ref[...] = x_ref[...] + _dot(a_ref[...], w_ref[...])


def residual_matmul(a, w, x, tm, tn, name):
    n, k = a.shape
    d = w.shape[1]
    return pl.pallas_call(
        _residual_matmul_kernel,
        grid=(n // tm, d // tn),
        in_specs=[
            pl.BlockSpec((tm, k), lambda i, j: (i, 0)),
            pl.BlockSpec((k, tn), lambda i, j: (0, j)),
            pl.BlockSpec((tm, tn), lambda i, j: (i, j)),
        ],
        out_specs=pl.BlockSpec((tm, tn), lambda i, j: (i, j)),
        out_shape=jax.ShapeDtypeStruct((n, d), F32),
        compiler_params=_cparams(("parallel", "parallel")),
        name=name,
    )(a, w, x)


def _merge_kernel(oa_ref, ob_ref, oc_ref, wa_ref, wb_ref, wc_ref, ga_ref, gb_ref, gc_ref, o_ref):
    merged = _sigmoid(ga_ref[...]) * _dot(oa_ref[...], wa_ref[...])
    merged += _sigmoid(gb_ref[...]) * _dot(ob_ref[...], wb_ref[...])
    merged += _sigmoid(gc_ref[...]) * _dot(oc_ref[...], wc_ref[...])
    o_ref[...] = merged.astype(BF16)


def merge_branches(o_a, o_b, o_c, wa, wb, wc, proj, tm=1024, tn=512):
    n = o_a.shape[0]
    gate_blk0 = GATE_COL0 // tn
    per_branch = D_MODEL // tn

    def gate_spec(br):
        return pl.BlockSpec((tm, tn), lambda i, j: (i, gate_blk0 + br * per_branch + j))

    return pl.pallas_call(
        _merge_kernel,
        grid=(n // tm, D_MODEL // tn),
        in_specs=[
            pl.BlockSpec((tm, A_WIDTH), lambda i, j: (i, 0)),
            pl.BlockSpec((tm, B_WIDTH), lambda i, j: (i, 0)),
            pl.BlockSpec((tm, C_WIDTH), lambda i, j: (i, 0)),
            pl.BlockSpec((A_WIDTH, tn), lambda i, j: (0, j)),
            pl.BlockSpec((B_WIDTH, tn), lambda i, j: (0, j)),
            pl.BlockSpec((C_WIDTH, tn), lambda i, j: (0, j)),
            gate_spec(0), gate_spec(1), gate_spec(2),
        ],
        out_specs=pl.BlockSpec((tm, tn), lambda i, j: (i, j)),
        out_shape=jax.ShapeDtypeStruct((n, D_MODEL), BF16),
        compiler_params=_cparams(("parallel", "parallel")),
        name="merge_branches",
    )(o_a, o_b, o_c, wa, wb, wc, proj, proj, proj)


def _rope_partial(x, cos, sin):
    half = ROPE_DIM // 2
    lane = lax.broadcasted_iota(jnp.int32, x.shape, 1)
    partner = jnp.where(lane < half, pltpu.roll(x, HEAD_DIM - half, 1), pltpu.roll(x, half, 1))
    return x * cos + partner * sin


def _rope_full(x, cos, sin):
    return x * cos + pltpu.roll(x, HEAD_DIM // 2, 1) * sin


def _rotary_tables(t, inv_freq, rot):
    pos = jnp.arange(t, dtype=F32)
    ang = pos[:, None] * inv_freq[None, :]
    ang = jnp.concatenate([ang, ang], axis=-1)
    cos, sin = jnp.cos(ang), jnp.sin(ang)
    sign = jnp.concatenate([-jnp.ones((rot // 2,), F32), jnp.ones((rot // 2,), F32)])
    pad = HEAD_DIM - rot
    cos = jnp.concatenate([cos, jnp.ones((t, pad), F32)], axis=-1)
    sin = jnp.concatenate([sin * sign[None, :], jnp.zeros((t, pad), F32)], axis=-1)
    return cos, sin


def _moba_kprep_kernel(k_ref, v_ref, g_ref, cos_ref, sin_ref, k3_ref, v3_ref, kmean_ref):
    nb = k3_ref.shape[0]
    for j in range(nb):
        rows = slice(j * MOBA_BLOCK, (j + 1) * MOBA_BLOCK)
        x = k_ref[rows, :]
        ms = jnp.mean(x * x, axis=-1, keepdims=True)
        kn = x * lax.rsqrt(ms + NORM_EPS) * g_ref[...]
        kr = _rope_partial(kn, cos_ref[rows, :], sin_ref[rows, :])
        k3_ref[j] = kr.astype(BF16)
        kmean_ref[0, j:j + 1, :] = jnp.mean(kr, axis=0, keepdims=True)
        v3_ref[j] = v_ref[rows, :].astype(BF16)


def moba_kprep(proj, k_gain, cos, sin, b, t):
    nb = t // MOBA_BLOCK
    bh = b * A_HEADS
    return pl.pallas_call(
        _moba_kprep_kernel,
        grid=(b, A_HEADS),
        in_specs=[
            pl.BlockSpec((t, HEAD_DIM), lambda i, h: (i, BLK_KA + h)),
            pl.BlockSpec((t, HEAD_DIM), lambda i, h: (i, BLK_VA + h)),
            pl.BlockSpec((1, HEAD_DIM), lambda i, h: (0, 0)),
            pl.BlockSpec((t, HEAD_DIM), lambda i, h: (0, 0)),
            pl.BlockSpec((t, HEAD_DIM), lambda i, h: (0, 0)),
        ],
        out_specs=[
            pl.BlockSpec((nb, MOBA_BLOCK, HEAD_DIM), lambda i, h: (i * A_HEADS + h, 0, 0)),
            pl.BlockSpec((nb, MOBA_BLOCK, HEAD_DIM), lambda i, h: (i * A_HEADS + h, 0, 0)),
            pl.BlockSpec((1, nb, HEAD_DIM), lambda i, h: (i * A_HEADS + h, 0, 0)),
        ],
        out_shape=[
            jax.ShapeDtypeStruct((bh * nb, MOBA_BLOCK, HEAD_DIM), BF16),
            jax.ShapeDtypeStruct((bh * nb, MOBA_BLOCK, HEAD_DIM), BF16),
            jax.ShapeDtypeStruct((bh, nb, HEAD_DIM), F32),
        ],
        compiler_params=_cparams(("parallel", "parallel")),
        name="moba_kprep",
    )(proj, proj, k_gain, cos, sin)


def _moba_attn_kernel(q_ref, g_ref, cos_ref, sin_ref, k3_ref, v3_ref, kmean_ref, o_ref,
                      qa_ref, m_ref, l_ref, acc_ref):
    nb = kmean_ref.shape[1]
    qi = pl.program_id(1)
    d = HEAD_DIM
    heads = range(A_HEADS)
    blk_rows = MOBA_BLOCK
    ones_cols = jnp.ones((blk_rows, d), BF16)
    lane = lax.broadcasted_iota(jnp.int32, (blk_rows, d), 1)

    def softmax_step(h, s, blk_idx, first):
        m_blk = jnp.max(s, axis=-1, keepdims=True)
        if first:
            m_new = jnp.broadcast_to(m_blk, (blk_rows, d))
        else:
            m_old = m_ref[h]
            m_new = jnp.maximum(m_old, m_blk)
            alpha = jnp.exp2(m_old - m_new)
        p = jnp.concatenate([jnp.exp2(s[:, :d] - m_new), jnp.exp2(s[:, d:] - m_new)], axis=1).astype(BF16)
        pv = _dot(p, jnp.concatenate([v3_ref[blk_idx], ones_cols], axis=1))
        m_ref[h] = m_new
        if first:
            l_ref[h] = pv[:, d:]
            acc_ref[h] = pv[:, :d]
        else:
            l_ref[h] = alpha * l_ref[h] + pv[:, d:]
            acc_ref[h] = alpha * acc_ref[h] + pv[:, :d]

    for h in heads:
        x = q_ref[:, h * d:(h + 1) * d]
        ms = jnp.mean(x * x, axis=-1, keepdims=True)
        qn = x * lax.rsqrt(ms + NORM_EPS) * g_ref[...]
        qr = _rope_partial(qn, cos_ref[...], sin_ref[...])

        gate = _dot_nt(kmean_ref[h], qr, precision=lax.Precision.HIGHEST)
        blk = lax.broadcasted_iota(jnp.int32, gate.shape, 0)
        past = blk < qi
        gate = jnp.where(past, gate, -jnp.inf)
        rank = jnp.zeros(gate.shape, F32)
        for i in range(nb):
            gi = gate[i:i + 1, :]
            wins = jnp.where(gi > gate, 1.0, jnp.where(gi == gate, jnp.where(blk > i, 1.0, 0.0), 0.0))
            rank = rank + wins
        bias = jnp.where(past, jnp.where(rank < MOBA_TOPK, 0.0, NEG_BIG), NEG_BIG)
        bias_q = jnp.concatenate([bias, jnp.zeros((d - nb, blk_rows), F32)], axis=0).T
        qs = qr * (d ** -0.5 * LOG2_E)
        qa_ref[h] = jnp.concatenate([qs.astype(BF16), bias_q.astype(BF16)], axis=1)

        s = _dot_nt(qa_ref[h][:, :d], k3_ref[h * nb + qi])
        q_pos = lax.broadcasted_iota(jnp.int32, s.shape, 0)
        key_pos = lax.broadcasted_iota(jnp.int32, s.shape, 1)
        softmax_step(h, jnp.where(key_pos <= q_pos, s, NEG_BIG), h * nb + qi, True)

    def body(j, carry):
        onehot = jnp.where(lane == j, 1.0, 0.0).astype(BF16)
        for h in heads:
            k_aug = jnp.concatenate([k3_ref[h * nb + j], onehot], axis=1)
            softmax_step(h, _dot_nt(qa_ref[h], k_aug), h * nb + j, False)
        return carry

    lax.fori_loop(0, qi, body, 0)
    for h in heads:
        o_ref[:, h * d:(h + 1) * d] = (acc_ref[h] / l_ref[h]).astype(BF16)


def moba_attention(proj, q_gain, cos, sin, k3, v3, kmean, b, t):
    nb = t // MOBA_BLOCK
    n = b * t
    hb = A_HEADS * nb
    stat = pltpu.VMEM((A_HEADS, MOBA_BLOCK, HEAD_DIM), F32)
    return pl.pallas_call(
        _moba_attn_kernel,
        grid=(b, nb),
        in_specs=[
            pl.BlockSpec((MOBA_BLOCK, A_WIDTH), lambda i, q: (i * nb + q, BLK_QA // A_HEADS)),
            pl.BlockSpec((1, HEAD_DIM), lambda i, q: (0, 0)),
            pl.BlockSpec((MOBA_BLOCK, HEAD_DIM), lambda i, q: (q, 0)),
            pl.BlockSpec((MOBA_BLOCK, HEAD_DIM), lambda i, q: (q, 0)),
            pl.BlockSpec((hb, MOBA_BLOCK, HEAD_DIM), lambda i, q: (i, 0, 0)),
            pl.BlockSpec((hb, MOBA_BLOCK, HEAD_DIM), lambda i, q: (i, 0, 0)),
            pl.BlockSpec((A_HEADS, nb, HEAD_DIM), lambda i, q: (i, 0, 0)),
        ],
        out_specs=pl.BlockSpec((MOBA_BLOCK, A_WIDTH), lambda i, q: (i * nb + q, 0)),
        out_shape=jax.ShapeDtypeStruct((n, A_WIDTH), BF16),
        scratch_shapes=[pltpu.VMEM((A_HEADS, MOBA_BLOCK, 2 * HEAD_DIM), BF16), stat, stat, stat],
        compiler_params=_cparams(("parallel", "arbitrary")),
        name="moba_attention",
    )(proj, q_gain, cos, sin, k3, v3, kmean)


def _retention_kernel(q_ref, k_ref, v_ref, gate_ref, cos_ref, sin_ref, decay_ref, zeta_ref, xi_ref,
                      gamma_ref, gain_ref, o_ref, state_ref):
    @pl.when(pl.program_id(2) == 0)
    def _():
        state_ref[...] = jnp.zeros_like(state_ref)

    c = RET_CHUNK
    decay = decay_ref[0]
    zeta = zeta_ref[0]
    xi = xi_ref[0]
    gamma_c = gamma_ref[0, 0:1, :]
    state = state_ref[...]
    for n in range(q_ref.shape[0] // c):
        rows = slice(n * c, (n + 1) * c)
        cos, sin = cos_ref[rows, :], sin_ref[rows, :]
        q = _rope_full(q_ref[rows, :], cos, sin)
        k = _rope_full(k_ref[rows, :], cos, sin) * (HEAD_DIM ** -0.5)
        qb, kb, vb = q.astype(BF16), k.astype(BF16), v_ref[rows, :].astype(BF16)
        scores = _dot_nt(qb, kb) * decay
        inner = _dot(scores.astype(BF16), vb)
        cross = _dot(qb, state.astype(BF16)) * xi
        ret = inner + cross
        state = state * gamma_c + _dot((k * zeta).T.astype(BF16), vb)
        ms = jnp.mean(ret * ret, axis=-1, keepdims=True)
        y = ret * lax.rsqrt(ms + NORM_EPS) * gain_ref[0]
        o_ref[rows, :] = (y * _silu(gate_ref[rows, :])).astype(BF16)
    state_ref[...] = state


def retention(proj, cos, sin, decay, zeta, xi, gamma, gain, b, t, rows=512):
    n = b * t
    steps = t // rows

    def col(blk0):
        return pl.BlockSpec((rows, HEAD_DIM), lambda i, h, s: (i * steps + s, blk0 + h))

    def per_head(shape):
        return pl.BlockSpec((1,) + shape, lambda i, h, s: (h, 0, 0))

    return pl.pallas_call(
        _retention_kernel,
        grid=(b, B_HEADS, steps),
        in_specs=[
            col(BLK_QB), col(BLK_KB), col(BLK_VB), col(BLK_GB),
            pl.BlockSpec((rows, HEAD_DIM), lambda i, h, s: (s, 0)),
            pl.BlockSpec((rows, HEAD_DIM), lambda i, h, s: (s, 0)),
            per_head((RET_CHUNK, RET_CHUNK)), per_head((RET_CHUNK, HEAD_DIM)), per_head((RET_CHUNK, HEAD_DIM)),
            per_head((8, HEAD_DIM)), per_head((1, HEAD_DIM)),
        ],
        out_specs=pl.BlockSpec((rows, HEAD_DIM), lambda i, h, s: (i * steps + s, h)),
        out_shape=jax.ShapeDtypeStruct((n, B_WIDTH), BF16),
        scratch_shapes=[pltpu.VMEM((HEAD_DIM, HEAD_DIM), F32)],
        compiler_params=_cparams(("parallel", "parallel", "arbitrary")),
        name="retention",
    )(proj, proj, proj, proj, cos, sin, decay, zeta, xi, gamma, gain)


def _retention_tables():
    c = RET_CHUNK
    log_gamma = jnp.log1p(-jnp.exp2(-5.0 - jnp.arange(B_HEADS, dtype=F32)))
    i = jnp.arange(c, dtype=F32)
    lg = log_gamma[:, None]
    tril = jnp.tril(jnp.ones((c, c), dtype=bool))
    rel = i[:, None] - i[None, :]
    decay = jnp.where(tril, jnp.exp(jnp.where(tril, lg[:, :, None] * rel, 0.0)), 0.0)
    zeta = jnp.exp(lg * (c - 1 - i))
    xi = jnp.exp(lg * (i + 1.0))
    gamma_c = jnp.exp(log_gamma * c)
    bcast = lambda v: jnp.broadcast_to(v[:, :, None], (B_HEADS, c, HEAD_DIM))
    gamma = jnp.broadcast_to(gamma_c[:, None, None], (B_HEADS, 8, HEAD_DIM))
    return decay, bcast(zeta), bcast(xi), gamma


def _lane_column(x, idx):
    lane = lax.broadcasted_iota(jnp.int32, x.shape, 1)
    col = jnp.sum(jnp.where(lane == idx, x, 0.0), axis=-1, keepdims=True)
    return jnp.broadcast_to(col, x.shape)


def _gdn_kernel(q_ref, k_ref, v_ref, qp_ref, kp_ref, vp_ref, z_ref, ba_ref, cw_ref, alog_ref, dtb_ref,
                gain_ref, o_ref, state_ref):
    r = GDN_GROUP
    c = GDN_CHUNK
    d = HEAD_DIM
    w = C_WIDTH
    hi = lax.Precision.HIGHEST
    first = pl.program_id(1) == 0

    @pl.when(first)
    def _():
        state_ref[...] = jnp.zeros_like(state_ref)

    row = lax.broadcasted_iota(jnp.int32, (r, r), 0)
    col = lax.broadcasted_iota(jnp.int32, (r, r), 1)
    same_chunk = (row // c) == (col // c)
    tril = same_chunk & (row >= col)
    strict = same_chunk & (row > col)
    lane_r = lax.broadcasted_iota(jnp.int32, (d, r), 1)

    def conv_silu(x_ref, prev_ref, w0):
        prev = jnp.where(first, 0.0, prev_ref[...])
        xe = jnp.concatenate([prev, x_ref[...]], axis=0)
        y = xe[8:, :] * cw_ref[CONV_WIDTH - 1:CONV_WIDTH, w0:w0 + w]
        for back in range(1, CONV_WIDTH):
            tap = CONV_WIDTH - 1 - back
            y = y + pltpu.roll(xe, back, 0)[8:, :] * cw_ref[tap:tap + 1, w0:w0 + w]
        return _silu(y)

    xq = conv_silu(q_ref, qp_ref, 0)
    xk = conv_silu(k_ref, kp_ref, w)
    xv = conv_silu(v_ref, vp_ref, 2 * w)

    ba = ba_ref[...]
    beta_all = _sigmoid(ba)
    pre = ba + dtb_ref[...]
    softplus = jnp.maximum(pre, 0.0) + jnp.log(1.0 + jnp.exp(-jnp.abs(pre)))
    g_all = -jnp.exp(alog_ref[...]) * softplus
    gcum_all = _dot(jnp.where(tril, 1.0, 0.0), g_all, precision=hi)
    gtot_all = _dot(jnp.where(same_chunk, 1.0, 0.0), g_all, precision=hi)

    for h in range(C_HEADS):
        cols = slice(h * d, (h + 1) * d)
        xqh, xkh, v = xq[:, cols], xk[:, cols], xv[:, cols]
        q = xqh * lax.rsqrt(jnp.sum(xqh * xqh, axis=-1, keepdims=True) + NORM_EPS) * (d ** -0.5)
        k = xkh * lax.rsqrt(jnp.sum(xkh * xkh, axis=-1, keepdims=True) + NORM_EPS)
        beta = _lane_column(beta_all, h)
        gcum = _lane_column(gcum_all, C_HEADS + h)
        gtot = _lane_column(gtot_all, C_HEADS + h)

        gcum_col = jnp.concatenate([gcum, gcum], axis=1)
        diff = gcum_col - gcum_col.T
        lmask = jnp.where(tril, jnp.exp(jnp.where(tril, diff, 0.0)), 0.0)
        exp_g = jnp.exp(gcum)

        k_beta = k * beta
        kb16 = k.astype(BF16)
        a_mat = jnp.where(strict, _dot_nt(k_beta.astype(BF16), kb16) * lmask, 0.0)
        attn = (_dot_nt(q.astype(BF16), kb16) * lmask).astype(BF16)

        x = jnp.concatenate([v * beta, k_beta * exp_g], axis=1)
        p = -a_mat
        steps = int(np.log2(c))
        for s in range(steps):
            p16 = p.astype(BF16)
            x = x + _dot(p16, x.astype(BF16))
            if s + 1 < steps:
                p = _dot(p16, p16)
        u = x[:, :d]
        w16 = x[:, d:].astype(BF16)
        qdec16 = (q * exp_g).astype(BF16)
        kdec_t = (k * jnp.exp(gtot - gcum)).T
        chunk_decay = jnp.exp(gtot)

        state = state_ref[h]
        vnew = []
        for n in range(r // c):
            rows = slice(n * c, (n + 1) * c)
            s16 = state.astype(BF16)
            v_new = u[rows, :] - _dot(w16[rows, :], s16)
            vnew.append(v_new.astype(BF16))
            vfull = jnp.concatenate(vnew + [jnp.zeros((c, d), BF16)] * (r // c - 1 - n), axis=0)
            o = _dot(qdec16[rows, :], s16) + _dot(attn[rows, :], vfull)
            kd = jnp.where((lane_r // c) == n, kdec_t, 0.0).astype(BF16)
            state = state * chunk_decay[n * c:n * c + 1, :] + _dot(kd, vfull)
            ms = jnp.mean(o * o, axis=-1, keepdims=True)
            y = o * lax.rsqrt(ms + NORM_EPS) * gain_ref[...]
            o_ref[rows, cols] = (y * _silu(z_ref[rows, cols])).astype(BF16)
        state_ref[h] = state


def gated_deltanet(proj, conv_w, alog, dtb, gain, b, t):
    n = b * t
    r = GDN_GROUP
    groups = t // r
    c_blk = BLK_QC // C_HEADS

    def cur(off):
        return pl.BlockSpec((r, C_WIDTH), lambda i, g: (i * groups + g, c_blk + off))

    def prev(off):
        return pl.BlockSpec((8, C_WIDTH), lambda i, g: (jnp.maximum((i * groups + g) * (r // 8) - 1, 0), c_blk + off))

    vec = pl.BlockSpec((1, HEAD_DIM), lambda i, g: (0, 0))
    return pl.pallas_call(
        _gdn_kernel,
        grid=(b, groups),
        in_specs=[
            cur(0), cur(1), cur(2), prev(0), prev(1), prev(2), cur(3),
            pl.BlockSpec((r, HEAD_DIM), lambda i, g: (i * groups + g, BLK_BA)),
            pl.BlockSpec((CONV_WIDTH, 3 * C_WIDTH), lambda i, g: (0, 0)),
            vec, vec, vec,
        ],
        out_specs=pl.BlockSpec((r, C_WIDTH), lambda i, g: (i * groups + g, 0)),
        out_shape=jax.ShapeDtypeStruct((n, C_WIDTH), BF16),
        scratch_shapes=[pltpu.VMEM((C_HEADS, HEAD_DIM, HEAD_DIM), F32)],
        compiler_params=_cparams(("parallel", "arbitrary")),
        name="gated_deltanet",
    )(proj, proj, proj, proj, proj, proj, proj, proj, conv_w, alog, dtb, gain)


def _pad_lanes(v, fill=0.0):
    return jnp.concatenate([v, jnp.full((HEAD_DIM - v.shape[0],), fill, F32)])[None, :]


def kernel(x, attn_norm, w_in, q_norm, k_norm, ret_norm, conv_w, a_log, dt_bias, gdn_norm,
           w_branch, w_out, ffn_norm, w_gate, w_up, w_down):
    b, t, d = x.shape
    n = b * t
    depth = w_in.shape[0]
    xf = x.reshape(n, d)

    rope_freq = ROPE_THETA ** (-jnp.arange(0, ROPE_DIM, 2, dtype=F32) / ROPE_DIM)
    ret_freq = RET_THETA ** (-jnp.linspace(0.0, 1.0, HEAD_DIM // 2, dtype=F32))
    cos_a, sin_a = _rotary_tables(t, rope_freq, ROPE_DIM)
    cos_b, sin_b = _rotary_tables(t, ret_freq, HEAD_DIM)
    decay, zeta, xi, gamma = _retention_tables()
    w_proj = w_in_prep(w_in)

    for layer in range(depth):
        proj = in_proj(xf, attn_norm[layer][None, :], w_proj, layer)

        k3, vt3, kmean = moba_kprep(proj, k_norm[layer][None, :], cos_a, sin_a, b, t)
        o_a = moba_attention(proj, q_norm[layer][None, :], cos_a, sin_a, k3, vt3,
                             kmean, b, t)

        o_b = retention(proj, cos_b, sin_b, decay, zeta, xi, gamma,
                        ret_norm[layer].reshape(B_HEADS, 1, HEAD_DIM), b, t)

        alog_v = jnp.concatenate([jnp.zeros((C_HEADS,), F32), a_log[layer]])
        dtb_v = jnp.concatenate([jnp.zeros((C_HEADS,), F32), dt_bias[layer]])
        o_c = gated_deltanet(proj, conv_w[layer], _pad_lanes(alog_v), _pad_lanes(dtb_v),
                             gdn_norm[layer][None, :], b, t)

        wbr = w_branch[layer].astype(BF16)
        merged = merge_branches(o_a, o_b, o_c, wbr[:A_WIDTH], wbr[A_WIDTH:A_WIDTH + B_WIDTH],
                                wbr[A_WIDTH + B_WIDTH:], proj)
        xf = residual_matmul(merged, w_out[layer].astype(BF16), xf, 1024, 512, "out_proj")

        hidden = ffn_up(xf, ffn_norm[layer][None, :], w_gate[layer].astype(BF16), w_up[layer].astype(BF16))
        xf = residual_matmul(hidden, w_down[layer].astype(BF16), xf, 512, 512, "ffn_down")
    return xf.reshape(b, t, d)
```

```python
import jax
import jax.numpy as jnp
import numpy as np
from jax import lax
from jax.experimental import pallas as pl
from jax.experimental.pallas import tpu as pltpu

F32 = jnp.float32
BF16 = jnp.bfloat16

D_MODEL = 2048
HEAD_DIM = 128
A_HEADS = 6
B_HEADS = 5
C_HEADS = 5
A_WIDTH = A_HEADS * HEAD_DIM
B_WIDTH = B_HEADS * HEAD_DIM
C_WIDTH = C_HEADS * HEAD_DIM
N_BRANCHES = 3
MOBA_BLOCK = 256
MOBA_TOPK = 3
ROPE_THETA = 500000.0
ROPE_DIM = HEAD_DIM // 4
RET_THETA = 10000.0
RET_CHUNK = 128
GDN_CHUNK = 64
CONV_WIDTH = 4
NORM_EPS = 1e-6

BLK_QA, BLK_KA, BLK_VA = 0, 6, 12
BLK_BA = 18
BLK_QC, BLK_KC, BLK_VC, BLK_ZC = 20, 25, 30, 35
BLK_QB, BLK_KB, BLK_VB, BLK_GB = 40, 45, 50, 55
BLK_GATE = 60
N_BA = 2 * C_HEADS
GATE_COL0 = BLK_GATE * HEAD_DIM
PROJ_BLOCKS = BLK_GATE + N_BRANCHES * D_MODEL // HEAD_DIM
SRC_BLK_B, SRC_BLK_C, SRC_BLK_BA = 18, 38, 58

GDN_GROUP = 4 * GDN_CHUNK
NEG_BIG = -1e30
LOG2_E = 1.4426950408889634
VMEM_LIMIT = 52 * 1024 * 1024


def _cparams(sem):
    return pltpu.CompilerParams(dimension_semantics=sem, vmem_limit_bytes=VMEM_LIMIT)


def _sigmoid(x):
    return 1.0 / (1.0 + jnp.exp(-x))


def _silu(x):
    return x * _sigmoid(x)


def _dot(a, b, precision=None):
    return jnp.dot(a, b, preferred_element_type=F32, precision=precision)


def _dot_nt(a, b, precision=None):
    return lax.dot_general(a, b, (((1,), (1,)), ((), ())), preferred_element_type=F32, precision=precision)


def _w_in_src_col(j):
    hd = HEAD_DIM
    return jnp.where(
        j < BLK_BA, j * hd,
        jnp.where(j < BLK_QC, SRC_BLK_BA * hd,
                  jnp.where(j < BLK_QB, (j - BLK_QC + SRC_BLK_C) * hd,
                            jnp.where(j < BLK_GATE, (j - BLK_QB + SRC_BLK_B) * hd,
                                      (j - BLK_GATE + SRC_BLK_BA) * hd + N_BA))))


def _w_in_prep_kernel(a_ref, o_ref):
    j = pl.program_id(0)
    depth = o_ref.shape[0]
    row = lax.broadcasted_iota(jnp.int32, (a_ref.shape[0], a_ref.shape[2]), 0)
    for l in range(depth):
        x = a_ref[:, l, :]
        x = jnp.where(j == BLK_BA, jnp.where(row < N_BA, x, 0.0), jnp.where(j == BLK_BA + 1, 0.0, x))
        o_ref[l] = x.T.astype(BF16)


def w_in_prep(w_in):
    depth, d, _ = w_in.shape
    return pl.pallas_call(
        _w_in_prep_kernel,
        grid=(PROJ_BLOCKS,),
        in_specs=[pl.BlockSpec((pl.Element(HEAD_DIM), pl.Element(depth), pl.Element(d)),
                               lambda j: (_w_in_src_col(j), 0, 0))],
        out_specs=pl.BlockSpec((depth, d, HEAD_DIM), lambda j: (0, 0, j)),
        out_shape=jax.ShapeDtypeStruct((depth, d, PROJ_BLOCKS * HEAD_DIM), BF16),
        compiler_params=_cparams(("parallel",)),
        name="w_in_prep",
    )(jnp.transpose(w_in, (2, 0, 1)))


def _norm_rows_to(h_ref, x_ref, g_ref, rows_per_step=256):
    n = x_ref.shape[0] // rows_per_step

    def body(r, c):
        rows = pl.ds(pl.multiple_of(r * rows_per_step, rows_per_step), rows_per_step)
        x = x_ref[rows, :]
        ms = jnp.mean(x * x, axis=-1, keepdims=True)
        h_ref[rows, :] = (x * lax.rsqrt(ms + NORM_EPS) * g_ref[...]).astype(BF16)
        return c

    lax.fori_loop(0, n, body, 0)


def _in_proj_kernel(x_ref, g_ref, w_ref, o_ref, h_ref):
    @pl.when(pl.program_id(1) == 0)
    def _():
        _norm_rows_to(h_ref, x_ref, g_ref)

    o_ref[...] = _dot(h_ref[...], w_ref[0])


def in_proj(x, gain, w, layer, tm=1024, tn=1536):
    n, d = x.shape
    cols = w.shape[2]
    return pl.pallas_call(
        _in_proj_kernel,
        grid=(n // tm, cols // tn),
        in_specs=[
            pl.BlockSpec((tm, d), lambda i, j: (i, 0)),
            pl.BlockSpec((1, d), lambda i, j: (0, 0)),
            pl.BlockSpec((1, d, tn), lambda i, j: (layer, 0, j)),
        ],
        out_specs=pl.BlockSpec((tm, tn), lambda i, j: (i, j)),
        out_shape=jax.ShapeDtypeStruct((n, cols), F32),
        scratch_shapes=[pltpu.VMEM((tm, d), BF16)],
        compiler_params=_cparams(("parallel", "arbitrary")),
        name="in_proj",
    )(x, gain, w)


def _ffn_up_kernel(x_ref, g_ref, wg_ref, wu_ref, o_ref, h_ref):
    @pl.when(pl.program_id(1) == 0)
    def _():
        _norm_rows_to(h_ref, x_ref, g_ref)

    h = h_ref[...]
    o_ref[...] = (_silu(_dot(h, wg_ref[...])) * _dot(h, wu_ref[...])).astype(BF16)


def ffn_up(x, gain, wg, wu, tm=1024, tn=512):
    n, d = x.shape
    cols = wg.shape[1]
    return pl.pallas_call(
        _ffn_up_kernel,
        grid=(n // tm, cols // tn),
        in_specs=[
            pl.BlockSpec((tm, d), lambda i, j: (i, 0)),
            pl.BlockSpec((1, d), lambda i, j: (0, 0)),
            pl.BlockSpec((d, tn), lambda i, j: (0, j)),
            pl.BlockSpec((d, tn), lambda i, j: (0, j)),
        ],
        out_specs=pl.BlockSpec((tm, tn), lambda i, j: (i, j)),
        out_shape=jax.ShapeDtypeStruct((n, cols), BF16),
        scratch_shapes=[pltpu.VMEM((tm, d), BF16)],
        compiler_params=_cparams(("parallel", "arbitrary")),
        name="ffn_up",
    )(x, gain, wg, wu)


def _residual_matmul_kernel(a_ref, w_ref, x_ref, o_ref):
    o_ref[...] = x_ref[...] + _dot(a_ref[...], w_ref[...])


def residual_matmul(a, w, x, tm, tn, name):
    n, k = a.shape
    d = w.shape[1]
    return pl.pallas_call(
        _residual_matmul_kernel,
        grid=(n // tm, d // tn),
        in_specs=[
            pl.BlockSpec((tm, k), lambda i, j: (i, 0)),
            pl.BlockSpec((k, tn), lambda i, j: (0, j)),
            pl.BlockSpec((tm, tn), lambda i, j: (i, j)),
        ],
        out_specs=pl.BlockSpec((tm, tn), lambda i, j: (i, j)),
        out_shape=jax.ShapeDtypeStruct((n, d), F32),
        compiler_params=_cparams(("parallel", "parallel")),
        name=name,
    )(a, w, x)


def _merge_kernel(oa_ref, ob_ref, oc_ref, wa_ref, wb_ref, wc_ref, ga_ref, gb_ref, gc_ref, o_ref):
    merged = _sigmoid(ga_ref[...]) * _dot(oa_ref[...], wa_ref[...])
    merged += _sigmoid(gb_ref[...]) * _dot(ob_ref[...], wb_ref[...])
    merged += _sigmoid(gc_ref[...]) * _dot(oc_ref[...], wc_ref[...])
    o_ref[...] = merged.astype(BF16)


def merge_branches(o_a, o_b, o_c, wa, wb, wc, proj, tm=1024, tn=512):
    n = o_a.shape[0]
    assert GATE_COL0 % tn == 0 and D_MODEL % tn == 0
    gate_blk0 = GATE_COL0 // tn
    per_branch = D_MODEL // tn

    def gate_spec(br):
        return pl.BlockSpec((tm, tn), lambda i, j: (i, gate_blk0 + br * per_branch + j))

    return pl.pallas_call(
        _merge_kernel,
        grid=(n // tm, D_MODEL // tn),
        in_specs=[
            pl.BlockSpec((tm, A_WIDTH), lambda i, j: (i, 0)),
            pl.BlockSpec((tm, B_WIDTH), lambda i, j: (i, 0)),
            pl.BlockSpec((tm, C_WIDTH), lambda i, j: (i, 0)),
            pl.BlockSpec((A_WIDTH, tn), lambda i, j: (0, j)),
            pl.BlockSpec((B_WIDTH, tn), lambda i, j: (0, j)),
            pl.BlockSpec((C_WIDTH, tn), lambda i, j: (0, j)),
            gate_spec(0), gate_spec(1), gate_spec(2),
        ],
        out_specs=pl.BlockSpec((tm, tn), lambda i, j: (i, j)),
        out_shape=jax.ShapeDtypeStruct((n, D_MODEL), BF16),
        compiler_params=_cparams(("parallel", "parallel")),
        name="merge_branches",
    )(o_a, o_b, o_c, wa, wb, wc, proj, proj, proj)


def _rope_partial(x, cos, sin):
    half = ROPE_DIM // 2
    lane = lax.broadcasted_iota(jnp.int32, x.shape, 1)
    partner = jnp.where(lane < half, pltpu.roll(x, HEAD_DIM - half, 1), pltpu.roll(x, half, 1))
    return x * cos + partner * sin


def _rope_full(x, cos, sin):
    return x * cos + pltpu.roll(x, HEAD_DIM // 2, 1) * sin


def _rotary_tables(t, inv_freq, rot):
    pos = jnp.arange(t, dtype=F32)
    ang = pos[:, None] * inv_freq[None, :]
    ang = jnp.concatenate([ang, ang], axis=-1)
    cos, sin = jnp.cos(ang), jnp.sin(ang)
    sign = jnp.concatenate([-jnp.ones((rot // 2,), F32), jnp.ones((rot // 2,), F32)])
    pad = HEAD_DIM - rot
    cos = jnp.concatenate([cos, jnp.ones((t, pad), F32)], axis=-1)
    sin = jnp.concatenate([sin * sign[None, :], jnp.zeros((t, pad), F32)], axis=-1)
    return cos, sin


def _moba_kprep_kernel(k_ref, v_ref, g_ref, cos_ref, sin_ref, k3_ref, v3_ref, kmean_ref):
    nb = k3_ref.shape[0]
    for j in range(nb):
        rows = slice(j * MOBA_BLOCK, (j + 1) * MOBA_BLOCK)
        x = k_ref[rows, :]
        ms = jnp.mean(x * x, axis=-1, keepdims=True)
        kn = x * lax.rsqrt(ms + NORM_EPS) * g_ref[...]
        kr = _rope_partial(kn, cos_ref[rows, :], sin_ref[rows, :])
        k3_ref[j] = kr.astype(BF16)
        kmean_ref[0, j:j + 1, :] = jnp.mean(kr, axis=0, keepdims=True)
        v3_ref[j] = v_ref[rows, :].astype(BF16)


def moba_kprep(proj, k_gain, cos, sin, b, t):
    nb = t // MOBA_BLOCK
    bh = b * A_HEADS
    return pl.pallas_call(
        _moba_kprep_kernel,
        grid=(b, A_HEADS),
        in_specs=[
            pl.BlockSpec((t, HEAD_DIM), lambda i, h: (i, BLK_KA + h)),
            pl.BlockSpec((t, HEAD_DIM), lambda i, h: (i, BLK_VA + h)),
            pl.BlockSpec((1, HEAD_DIM), lambda i, h: (0, 0)),
            pl.BlockSpec((t, HEAD_DIM), lambda i, h: (0, 0)),
            pl.BlockSpec((t, HEAD_DIM), lambda i, h: (0, 0)),
        ],
        out_specs=[
            pl.BlockSpec((nb, MOBA_BLOCK, HEAD_DIM), lambda i, h: (i * A_HEADS + h, 0, 0)),
            pl.BlockSpec((nb, MOBA_BLOCK, HEAD_DIM), lambda i, h: (i * A_HEADS + h, 0, 0)),
            pl.BlockSpec((1, nb, HEAD_DIM), lambda i, h: (i * A_HEADS + h, 0, 0)),
        ],
        out_shape=[
            jax.ShapeDtypeStruct((bh * nb, MOBA_BLOCK, HEAD_DIM), BF16),
            jax.ShapeDtypeStruct((bh * nb, MOBA_BLOCK, HEAD_DIM), BF16),
            jax.ShapeDtypeStruct((bh, nb, HEAD_DIM), F32),
        ],
        compiler_params=_cparams(("parallel", "parallel")),
        name="moba_kprep",
    )(proj, proj, k_gain, cos, sin)


def _moba_attn_kernel(q_ref, g_ref, cos_ref, sin_ref, k3_ref, v3_ref, kmean_ref, o_ref,
                      qa_ref, m_ref, l_ref, acc_ref):
    nb = kmean_ref.shape[1]
    qi = pl.program_id(1)
    d = HEAD_DIM
    heads = range(A_HEADS)
    blk_rows = MOBA_BLOCK
    ones_cols = jnp.ones((blk_rows, d), BF16)
    lane = lax.broadcasted_iota(jnp.int32, (blk_rows, d), 1)

    def probs(h, s, first):
        m_blk = jnp.max(s, axis=-1, keepdims=True)
        if first:
            m_new, alpha = jnp.broadcast_to(m_blk, (blk_rows, d)), None
        else:
            m_old = m_ref[h]
            m_new = jnp.maximum(m_old, m_blk)
            alpha = jnp.exp2(m_old - m_new)
        m_ref[h] = m_new
        p = jnp.concatenate([jnp.exp2(s[:, :d] - m_new), jnp.exp2(s[:, d:] - m_new)], axis=1)
        return p.astype(BF16), alpha

    def accumulate(h, p, alpha, blk_idx):
        pv = _dot(p, jnp.concatenate([v3_ref[blk_idx], ones_cols], axis=1))
        if alpha is None:
            l_ref[h] = pv[:, d:]
            acc_ref[h] = pv[:, :d]
        else:
            l_ref[h] = alpha * l_ref[h] + pv[:, d:]
            acc_ref[h] = alpha * acc_ref[h] + pv[:, :d]

    def pipelined(score, first, blk_of):
        ahead = 3
        s = {h: score(h) for h in range(ahead)}
        for h in heads:
            p, alpha = probs(h, s.pop(h), first)
            if h + ahead < A_HEADS:
                s[h + ahead] = score(h + ahead)
            accumulate(h, p, alpha, blk_of(h))

    qr = []
    for h in heads:
        x = q_ref[:, h * d:(h + 1) * d]
        ms = jnp.mean(x * x, axis=-1, keepdims=True)
        qn = x * lax.rsqrt(ms + NORM_EPS) * g_ref[...]
        qr.append(_rope_partial(qn, cos_ref[...], sin_ref[...]))
    gates = [_dot_nt(kmean_ref[h], qr[h], precision=lax.Precision.HIGHEST) for h in heads]
    blk = lax.broadcasted_iota(jnp.int32, gates[0].shape, 0)
    past = blk < qi
    for h in heads:
        gate = jnp.where(past, gates[h], -jnp.inf)
        rank = jnp.zeros(gate.shape, F32)
        for i in range(nb):
            gi = gate[i:i + 1, :]
            wins = jnp.where(gi > gate, 1.0, jnp.where(gi == gate, jnp.where(blk > i, 1.0, 0.0), 0.0))
            rank = rank + wins
        bias = jnp.where(past, jnp.where(rank < MOBA_TOPK, 0.0, NEG_BIG), NEG_BIG)
        bias_q = jnp.concatenate([bias, jnp.zeros((d - nb, blk_rows), F32)], axis=0).T
        qs = qr[h] * (d ** -0.5 * LOG2_E)
        qa_ref[h] = jnp.concatenate([qs.astype(BF16), bias_q.astype(BF16)], axis=1)

    q_pos = lax.broadcasted_iota(jnp.int32, (blk_rows, blk_rows), 0)
    key_pos = lax.broadcasted_iota(jnp.int32, (blk_rows, blk_rows), 1)
    causal = key_pos <= q_pos

    def own_score(h):
        return jnp.where(causal, _dot_nt(qa_ref[h][:, :d], k3_ref[h * nb + qi]), NEG_BIG)

    pipelined(own_score, True, lambda h: h * nb + qi)

    def body(j, carry):
        onehot = jnp.where(lane == j, 1.0, 0.0).astype(BF16)

        def past_score(h):
            return _dot_nt(qa_ref[h], jnp.concatenate([k3_ref[h * nb + j], onehot], axis=1))

        pipelined(past_score, False, lambda h: h * nb + j)
        return carry

    lax.fori_loop(0, qi, body, 0)
    for h in heads:
        o_ref[:, h * d:(h + 1) * d] = (acc_ref[h] / l_ref[h]).astype(BF16)


def moba_attention(proj, q_gain, cos, sin, k3, v3, kmean, b, t):
    nb = t // MOBA_BLOCK
    n = b * t
    hb = A_HEADS * nb
    stat = pltpu.VMEM((A_HEADS, MOBA_BLOCK, HEAD_DIM), F32)
    return pl.pallas_call(
        _moba_attn_kernel,
        grid=(b, nb),
        in_specs=[
            pl.BlockSpec((MOBA_BLOCK, A_WIDTH), lambda i, q: (i * nb + q, BLK_QA // A_HEADS)),
            pl.BlockSpec((1, HEAD_DIM), lambda i, q: (0, 0)),
            pl.BlockSpec((MOBA_BLOCK, HEAD_DIM), lambda i, q: (q, 0)),
            pl.BlockSpec((MOBA_BLOCK, HEAD_DIM), lambda i, q: (q, 0)),
            pl.BlockSpec((hb, MOBA_BLOCK, HEAD_DIM), lambda i, q: (i, 0, 0)),
            pl.BlockSpec((hb, MOBA_BLOCK, HEAD_DIM), lambda i, q: (i, 0, 0)),
            pl.BlockSpec((A_HEADS, nb, HEAD_DIM), lambda i, q: (i, 0, 0)),
        ],
        out_specs=pl.BlockSpec((MOBA_BLOCK, A_WIDTH), lambda i, q: (i * nb + q, 0)),
        out_shape=jax.ShapeDtypeStruct((n, A_WIDTH), BF16),
        scratch_shapes=[pltpu.VMEM((A_HEADS, MOBA_BLOCK, 2 * HEAD_DIM), BF16), stat, stat, stat],
        compiler_params=_cparams(("parallel", "arbitrary")),
        name="moba_attention",
    )(proj, q_gain, cos, sin, k3, v3, kmean)


def _retention_kernel(q_ref, k_ref, v_ref, gate_ref, cos_ref, sin_ref, decay_ref, zeta_ref, xi_ref,
                      gamma_ref, gain_ref, o_ref, state_ref):
    @pl.when(pl.program_id(1) == 0)
    def _():
        state_ref[...] = jnp.zeros_like(state_ref)

    c = RET_CHUNK
    d = HEAD_DIM
    heads = range(B_HEADS)
    hcols = [slice(h * d, (h + 1) * d) for h in heads]
    state = [state_ref[h] for h in heads]
    for n in range(q_ref.shape[0] // c):
        rows = slice(n * c, (n + 1) * c)
        cos, sin = cos_ref[rows, :], sin_ref[rows, :]
        k = [_rope_full(k_ref[rows, hcols[h]], cos, sin) * (d ** -0.5) for h in heads]
        qb = [_rope_full(q_ref[rows, hcols[h]], cos, sin).astype(BF16) for h in heads]
        kb = [k[h].astype(BF16) for h in heads]
        vb = [v_ref[rows, hcols[h]].astype(BF16) for h in heads]
        scores = [_dot_nt(qb[h], kb[h]) for h in heads]
        cross = [_dot(qb[h], state[h].astype(BF16)) for h in heads]
        kz_t = [(k[h] * zeta_ref[h]).T.astype(BF16) for h in heads]
        inner = [_dot((scores[h] * decay_ref[h]).astype(BF16), vb[h]) for h in heads]
        update = [_dot(kz_t[h], vb[h]) for h in heads]
        for h in heads:
            state[h] = state[h] * gamma_ref[h, 0:1, :] + update[h]
            ret = inner[h] + cross[h] * xi_ref[h]
            ms = jnp.mean(ret * ret, axis=-1, keepdims=True)
            y = ret * lax.rsqrt(ms + NORM_EPS) * gain_ref[h]
            o_ref[rows, hcols[h]] = (y * _silu(gate_ref[rows, hcols[h]])).astype(BF16)
    for h in heads:
        state_ref[h] = state[h]


def retention(proj, cos, sin, decay, zeta, xi, gamma, gain, b, t, rows=512):
    n = b * t
    steps = t // rows
    b_blk = BLK_QB // B_HEADS

    def cur(off):
        return pl.BlockSpec((rows, B_WIDTH), lambda i, s: (i * steps + s, b_blk + off))

    def table(shape):
        return pl.BlockSpec((B_HEADS,) + shape, lambda i, s: (0, 0, 0))

    return pl.pallas_call(
        _retention_kernel,
        grid=(b, steps),
        in_specs=[
            cur(0), cur(1), cur(2), cur(3),
            pl.BlockSpec((rows, HEAD_DIM), lambda i, s: (s, 0)),
            pl.BlockSpec((rows, HEAD_DIM), lambda i, s: (s, 0)),
            table((RET_CHUNK, RET_CHUNK)), table((RET_CHUNK, HEAD_DIM)), table((RET_CHUNK, HEAD_DIM)),
            table((8, HEAD_DIM)), table((1, HEAD_DIM)),
        ],
        out_specs=pl.BlockSpec((rows, B_WIDTH), lambda i, s: (i * steps + s, 0)),
        out_shape=jax.ShapeDtypeStruct((n, B_WIDTH), BF16),
        scratch_shapes=[pltpu.VMEM((B_HEADS, HEAD_DIM, HEAD_DIM), F32)],
        compiler_params=_cparams(("parallel", "arbitrary")),
        name="retention",
    )(proj, proj, proj, proj, cos, sin, decay, zeta, xi, gamma, gain)


def _retention_tables():
    c = RET_CHUNK
    log_gamma = jnp.log1p(-jnp.exp2(-5.0 - jnp.arange(B_HEADS, dtype=F32)))
    i = jnp.arange(c, dtype=F32)
    lg = log_gamma[:, None]
    tril = jnp.tril(jnp.ones((c, c), dtype=bool))
    rel = i[:, None] - i[None, :]
    decay = jnp.where(tril, jnp.exp(jnp.where(tril, lg[:, :, None] * rel, 0.0)), 0.0)
    zeta = jnp.exp(lg * (c - 1 - i))
    xi = jnp.exp(lg * (i + 1.0))
    gamma_c = jnp.exp(log_gamma * c)
    bcast = lambda v: jnp.broadcast_to(v[:, :, None], (B_HEADS, c, HEAD_DIM))
    gamma = jnp.broadcast_to(gamma_c[:, None, None], (B_HEADS, 8, HEAD_DIM))
    return decay, bcast(zeta), bcast(xi), gamma


def _lane_column(x, idx):
    lane = lax.broadcasted_iota(jnp.int32, x.shape, 1)
    col = jnp.sum(jnp.where(lane == idx, x, 0.0), axis=-1, keepdims=True)
    return jnp.broadcast_to(col, x.shape)


def _gdn_kernel(q_ref, k_ref, v_ref, qp_ref, kp_ref, vp_ref, z_ref, ba_ref, cw_ref, alog_ref, dtb_ref,
                gain_ref, o_ref, state_ref):
    r = GDN_GROUP
    c = GDN_CHUNK
    d = HEAD_DIM
    w = C_WIDTH
    hi = lax.Precision.HIGHEST
    first = pl.program_id(1) == 0

    @pl.when(first)
    def _():
        state_ref[...] = jnp.zeros_like(state_ref)

    row = lax.broadcasted_iota(jnp.int32, (r, r), 0)
    col = lax.broadcasted_iota(jnp.int32, (r, r), 1)
    same_chunk = (row // c) == (col // c)
    tril = same_chunk & (row >= col)
    strict = same_chunk & (row > col)
    lane_r = lax.broadcasted_iota(jnp.int32, (d, r), 1)

    def conv_silu(x_ref, prev_ref, w0):
        prev = jnp.where(first, 0.0, prev_ref[...])
        xe = jnp.concatenate([prev, x_ref[...]], axis=0)
        y = xe[8:, :] * cw_ref[CONV_WIDTH - 1:CONV_WIDTH, w0:w0 + w]
        for back in range(1, CONV_WIDTH):
            tap = CONV_WIDTH - 1 - back
            y = y + pltpu.roll(xe, back, 0)[8:, :] * cw_ref[tap:tap + 1, w0:w0 + w]
        return _silu(y)

    xq = conv_silu(q_ref, qp_ref, 0)
    xk = conv_silu(k_ref, kp_ref, w)
    xv = conv_silu(v_ref, vp_ref, 2 * w)

    ba = ba_ref[...]
    beta_all = _sigmoid(ba)
    pre = ba + dtb_ref[...]
    softplus = jnp.maximum(pre, 0.0) + jnp.log(1.0 + jnp.exp(-jnp.abs(pre)))
    g_all = -jnp.exp(alog_ref[...]) * softplus
    gcum_all = _dot(jnp.where(tril, 1.0, 0.0), g_all, precision=hi)
    gtot_all = _dot(jnp.where(same_chunk, 1.0, 0.0), g_all, precision=hi)

    heads = range(C_HEADS)
    hcols = [slice(h * d, (h + 1) * d) for h in heads]
    q, k, k_beta, kb16, lmask, exp_g, gcum, gtot, x = ([None] * C_HEADS for _ in range(9))
    for h in heads:
        xqh, xkh, v = xq[:, hcols[h]], xk[:, hcols[h]], xv[:, hcols[h]]
        q[h] = xqh * lax.rsqrt(jnp.sum(xqh * xqh, axis=-1, keepdims=True) + NORM_EPS) * (d ** -0.5)
        k[h] = xkh * lax.rsqrt(jnp.sum(xkh * xkh, axis=-1, keepdims=True) + NORM_EPS)
        beta = _lane_column(beta_all, h)
        gcum[h] = _lane_column(gcum_all, C_HEADS + h)
        gtot[h] = _lane_column(gtot_all, C_HEADS + h)
        gcum_col = jnp.concatenate([gcum[h], gcum[h]], axis=1)
        diff = gcum_col - gcum_col.T
        lmask[h] = jnp.where(tril, jnp.exp(jnp.where(tril, diff, 0.0)), 0.0)
        exp_g[h] = jnp.exp(gcum[h])
        k_beta[h] = k[h] * beta
        kb16[h] = k[h].astype(BF16)
        x[h] = jnp.concatenate([v * beta, k_beta[h] * exp_g[h]], axis=1)

    kk = [_dot_nt(k_beta[h].astype(BF16), kb16[h]) for h in heads]
    qk = [_dot_nt(q[h].astype(BF16), kb16[h]) for h in heads]
    p = [-jnp.where(strict, kk[h] * lmask[h], 0.0) for h in heads]
    attn = [(qk[h] * lmask[h]).astype(BF16) for h in heads]

    steps = int(np.log2(c))
    for s in range(steps):
        p16 = [p[h].astype(BF16) for h in heads]
        x = [x[h] + _dot(p16[h], x[h].astype(BF16)) for h in heads]
        if s + 1 < steps:
            p = [_dot(p16[h], p16[h]) for h in heads]

    u = [x[h][:, :d] for h in heads]
    w16 = [x[h][:, d:].astype(BF16) for h in heads]
    qdec16 = [(q[h] * exp_g[h]).astype(BF16) for h in heads]
    kdec_t = [(k[h] * jnp.exp(gtot[h] - gcum[h])).T for h in heads]
    chunk_decay = [jnp.exp(gtot[h]) for h in heads]

    state = [state_ref[h] for h in heads]
    vnew = [[] for _ in heads]
    for n in range(r // c):
        rows = slice(n * c, (n + 1) * c)
        s16 = [state[h].astype(BF16) for h in heads]
        ws = [_dot(w16[h][rows, :], s16[h]) for h in heads]
        qs = [_dot(qdec16[h][rows, :], s16[h]) for h in heads]
        vfull = []
        for h in heads:
            vnew[h].append((u[h][rows, :] - ws[h]).astype(BF16))
            vfull.append(jnp.concatenate(vnew[h] + [jnp.zeros((c, d), BF16)] * (r // c - 1 - n), axis=0))
        av = [_dot(attn[h][rows, :], vfull[h]) for h in heads]
        kd = [jnp.where((lane_r // c) == n, kdec_t[h], 0.0).astype(BF16) for h in heads]
        kv = [_dot(kd[h], vfull[h]) for h in heads]
        for h in heads:
            state[h] = state[h] * chunk_decay[h][n * c:n * c + 1, :] + kv[h]
            o = qs[h] + av[h]
            ms = jnp.mean(o * o, axis=-1, keepdims=True)
            y = o * lax.rsqrt(ms + NORM_EPS) * gain_ref[...]
            o_ref[rows, hcols[h]] = (y * _silu(z_ref[rows, hcols[h]])).astype(BF16)
    for h in heads:
        state_ref[h] = state[h]


def gated_deltanet(proj, conv_w, alog, dtb, gain, b, t):
    n = b * t
    r = GDN_GROUP
    groups = t // r
    c_blk = BLK_QC // C_HEADS

    def cur(off):
        return pl.BlockSpec((r, C_WIDTH), lambda i, g: (i * groups + g, c_blk + off))

    def prev(off):
        return pl.BlockSpec((8, C_WIDTH), lambda i, g: (jnp.maximum((i * groups + g) * (r // 8) - 1, 0), c_blk + off))

    vec = pl.BlockSpec((1, HEAD_DIM), lambda i, g: (0, 0))
    return pl.pallas_call(
        _gdn_kernel,
        grid=(b, groups),
        in_specs=[
            cur(0), cur(1), cur(2), prev(0), prev(1), prev(2), cur(3),
            pl.BlockSpec((r, HEAD_DIM), lambda i, g: (i * groups + g, BLK_BA)),
            pl.BlockSpec((CONV_WIDTH, 3 * C_WIDTH), lambda i, g: (0, 0)),
            vec, vec, vec,
        ],
        out_specs=pl.BlockSpec((r, C_WIDTH), lambda i, g: (i * groups + g, 0)),
        out_shape=jax.ShapeDtypeStruct((n, C_WIDTH), BF16),
        scratch_shapes=[pltpu.VMEM((C_HEADS, HEAD_DIM, HEAD_DIM), F32)],
        compiler_params=_cparams(("parallel", "arbitrary")),
        name="gated_deltanet",
    )(proj, proj, proj, proj, proj, proj, proj, proj, conv_w, alog, dtb, gain)


def _pad_lanes(v, fill=0.0):
    return jnp.concatenate([v, jnp.full((HEAD_DIM - v.shape[0],), fill, F32)])[None, :]


def kernel(x, attn_norm, w_in, q_norm, k_norm, ret_norm, conv_w, a_log, dt_bias, gdn_norm,
           w_branch, w_out, ffn_norm, w_gate, w_up, w_down):
    b, t, d = x.shape
    n = b * t
    depth = w_in.shape[0]
    xf = x.reshape(n, d)

    rope_freq = ROPE_THETA ** (-jnp.arange(0, ROPE_DIM, 2, dtype=F32) / ROPE_DIM)
    ret_freq = RET_THETA ** (-jnp.linspace(0.0, 1.0, HEAD_DIM // 2, dtype=F32))
    cos_a, sin_a = _rotary_tables(t, rope_freq, ROPE_DIM)
    cos_b, sin_b = _rotary_tables(t, ret_freq, HEAD_DIM)
    decay, zeta, xi, gamma = _retention_tables()
    w_proj = w_in_prep(w_in)

    for layer in range(depth):
        proj = in_proj(xf, attn_norm[layer][None, :], w_proj, layer)

        k3, vt3, kmean = moba_kprep(proj, k_norm[layer][None, :], cos_a, sin_a, b, t)
        o_a = moba_attention(proj, q_norm[layer][None, :], cos_a, sin_a, k3, vt3,
                             kmean, b, t)

        o_b = retention(proj, cos_b, sin_b, decay, zeta, xi, gamma,
                        ret_norm[layer].reshape(B_HEADS, 1, HEAD_DIM), b, t)

        alog_v = jnp.concatenate([jnp.zeros((C_HEADS,), F32), a_log[layer]])
        dtb_v = jnp.concatenate([jnp.zeros((C_HEADS,), F32), dt_bias[layer]])
        o_c = gated_deltanet(proj, conv_w[layer], _pad_lanes(alog_v), _pad_lanes(dtb_v),
                             gdn_norm[layer][None, :], b, t)

        wbr = w_branch[layer].astype(BF16)
        merged = merge_branches(o_a, o_b, o_c, wbr[:A_WIDTH], wbr[A_WIDTH:A_WIDTH + B_WIDTH],
                                wbr[A_WIDTH + B_WIDTH:], proj)
        xf = residual_matmul(merged, w_out[layer].astype(BF16), xf, 1024, 1024, "out_proj")

        hidden = ffn_up(xf, ffn_norm[layer][None, :], w_gate[layer].astype(BF16), w_up[layer].astype(BF16))
        xf = residual_matmul(hidden, w_down[layer].astype(BF16), xf, 1024, 512, "ffn_down")
    return xf.reshape(b, t, d)
```

```python
import functools

import jax
import jax.numpy as jnp
import numpy as np
from jax import lax
from jax.experimental import pallas as pl
from jax.experimental.pallas import tpu as pltpu

F32 = jnp.float32
BF16 = jnp.bfloat16

D_MODEL = 2048
HEAD_DIM = 128
A_HEADS = 6
B_HEADS = 5
C_HEADS = 5
A_WIDTH = A_HEADS * HEAD_DIM
B_WIDTH = B_HEADS * HEAD_DIM
C_WIDTH = C_HEADS * HEAD_DIM
N_BRANCHES = 3
MOBA_BLOCK = 256
MOBA_TOPK = 3
ROPE_THETA = 500000.0
ROPE_DIM = HEAD_DIM // 4
RET_THETA = 10000.0
RET_CHUNK = 128
GDN_CHUNK = 64
CONV_WIDTH = 4
NORM_EPS = 1e-6

BLK_QA, BLK_KA, BLK_VA = 0, 6, 12
BLK_BA = 18
BLK_QC, BLK_KC, BLK_VC, BLK_ZC = 20, 25, 30, 35
BLK_QB, BLK_KB, BLK_VB, BLK_GB = 40, 45, 50, 55
BLK_GATE = 60
N_BA = 2 * C_HEADS
GATE_COL0 = BLK_GATE * HEAD_DIM
PROJ_BLOCKS = BLK_GATE + N_BRANCHES * D_MODEL // HEAD_DIM
SRC_BLK_B, SRC_BLK_C, SRC_BLK_BA = 18, 38, 58

GDN_GROUP = 4 * GDN_CHUNK
NEG_BIG = -1e30
LOG2_E = 1.4426950408889634
VMEM_LIMIT = 52 * 1024 * 1024


def _cparams(sem):
    return pltpu.CompilerParams(dimension_semantics=sem, vmem_limit_bytes=VMEM_LIMIT)


def _sigmoid(x):
    return 1.0 / (1.0 + jnp.exp(-x))


def _silu(x):
    return x * _sigmoid(x)


def _dot(a, b, precision=None):
    return jnp.dot(a, b, preferred_element_type=F32, precision=precision)


def _dot_nt(a, b, precision=None):
    return lax.dot_general(a, b, (((1,), (1,)), ((), ())), preferred_element_type=F32, precision=precision)


def _w_in_src_col(j):
    hd = HEAD_DIM
    return jnp.where(
        j < BLK_BA, j * hd,
        jnp.where(j < BLK_QC, SRC_BLK_BA * hd,
                  jnp.where(j < BLK_QB, (j - BLK_QC + SRC_BLK_C) * hd,
                            jnp.where(j < BLK_GATE, (j - BLK_QB + SRC_BLK_B) * hd,
                                      (j - BLK_GATE + SRC_BLK_BA) * hd + N_BA))))


def _w_in_prep_kernel(a_ref, o_ref):
    j = pl.program_id(0)
    depth = o_ref.shape[0]
    special = (j == BLK_BA) | (j == BLK_BA + 1)

    @pl.when(jnp.logical_not(special))
    def _():
        for l in range(depth):
            o_ref[l] = a_ref[:, l, :].T.astype(BF16)

    @pl.when(special)
    def _():
        row = lax.broadcasted_iota(jnp.int32, (a_ref.shape[0], a_ref.shape[2]), 0)
        keep = (row < N_BA) & (j == BLK_BA)
        for l in range(depth):
            o_ref[l] = jnp.where(keep, a_ref[:, l, :], 0.0).T.astype(BF16)


def w_in_prep(w_in):
    depth, d, _ = w_in.shape
    return pl.pallas_call(
        _w_in_prep_kernel,
        grid=(PROJ_BLOCKS,),
        in_specs=[pl.BlockSpec((pl.Element(HEAD_DIM), pl.Element(depth), pl.Element(d)),
                               lambda j: (_w_in_src_col(j), 0, 0))],
        out_specs=pl.BlockSpec((depth, d, HEAD_DIM), lambda j: (0, 0, j)),
        out_shape=jax.ShapeDtypeStruct((depth, d, PROJ_BLOCKS * HEAD_DIM), BF16),
        compiler_params=_cparams(("parallel",)),
        name="w_in_prep",
    )(jnp.transpose(w_in, (2, 0, 1)))


def _norm_rows_to(h_ref, x_ref, g_ref, rows_per_step=256):
    n = x_ref.shape[0] // rows_per_step

    def body(r, c):
        rows = pl.ds(pl.multiple_of(r * rows_per_step, rows_per_step), rows_per_step)
        x = x_ref[rows, :]
        ms = jnp.mean(x * x, axis=-1, keepdims=True)
        h_ref[rows, :] = (x * lax.rsqrt(ms + NORM_EPS) * g_ref[...]).astype(BF16)
        return c

    lax.fori_loop(0, n, body, 0)


def _in_proj_kernel(x_ref, g_ref, w_ref, o_ref, gate_ref, h_ref, *, main_steps):
    j = pl.program_id(1)

    @pl.when(j == 0)
    def _():
        _norm_rows_to(h_ref, x_ref, g_ref)

    @pl.when(j < main_steps)
    def _():
        o_ref[...] = _dot(h_ref[...], w_ref[0])

    @pl.when(j >= main_steps)
    def _():
        gate_ref[...] = _sigmoid(_dot(h_ref[...], w_ref[0])).astype(BF16)


def in_proj(x, gain, w, layer, tm=1024, tn=768):
    n, d = x.shape
    cols = w.shape[2]
    assert GATE_COL0 % tn == 0 and cols % tn == 0
    main_steps = GATE_COL0 // tn
    return pl.pallas_call(
        functools.partial(_in_proj_kernel, main_steps=main_steps),
        grid=(n // tm, cols // tn),
        in_specs=[
            pl.BlockSpec((tm, d), lambda i, j: (i, 0)),
            pl.BlockSpec((1, d), lambda i, j: (0, 0)),
            pl.BlockSpec((1, d, tn), lambda i, j: (layer, 0, j)),
        ],
        out_specs=[
            pl.BlockSpec((tm, tn), lambda i, j: (i, jnp.minimum(j, main_steps - 1))),
            pl.BlockSpec((tm, tn), lambda i, j: (i, jnp.maximum(j - main_steps, 0))),
        ],
        out_shape=[
            jax.ShapeDtypeStruct((n, GATE_COL0), F32),
            jax.ShapeDtypeStruct((n, cols - GATE_COL0), BF16),
        ],
        scratch_shapes=[pltpu.VMEM((tm, d), BF16)],
        compiler_params=_cparams(("parallel", "arbitrary")),
        name="in_proj",
    )(x, gain, w)


def _ffn_up_kernel(x_ref, g_ref, wg_ref, wu_ref, o_ref, h_ref):
    @pl.when(pl.program_id(1) == 0)
    def _():
        _norm_rows_to(h_ref, x_ref, g_ref)

    h = h_ref[...]
    o_ref[...] = (_silu(_dot(h, wg_ref[...])) * _dot(h, wu_ref[...])).astype(BF16)


def ffn_up(x, gain, wg, wu, tm=1024, tn=512):
    n, d = x.shape
    cols = wg.shape[1]
    return pl.pallas_call(
        _ffn_up_kernel,
        grid=(n // tm, cols // tn),
        in_specs=[
            pl.BlockSpec((tm, d), lambda i, j: (i, 0)),
            pl.BlockSpec((1, d), lambda i, j: (0, 0)),
            pl.BlockSpec((d, tn), lambda i, j: (0, j)),
            pl.BlockSpec((d, tn), lambda i, j: (0, j)),
        ],
        out_specs=pl.BlockSpec((tm, tn), lambda i, j: (i, j)),
        out_shape=jax.ShapeDtypeStruct((n, cols), BF16),
        scratch_shapes=[pltpu.VMEM((tm, d), BF16)],
        compiler_params=_cparams(("parallel", "arbitrary")),
        name="ffn_up",
    )(x, gain, wg, wu)


def _residual_matmul_kernel(a_ref, w_ref, x_ref, o_ref):
    o_ref[...] = x_ref[...] + _dot(a_ref[...], w_ref[...])


def residual_matmul(a, w, x, tm, tn, name):
    n, k = a.shape
    d = w.shape[1]
    return pl.pallas_call(
        _residual_matmul_kernel,
        grid=(n // tm, d // tn),
        in_specs=[
            pl.BlockSpec((tm, k), lambda i, j: (i, 0)),
            pl.BlockSpec((k, tn), lambda i, j: (0, j)),
            pl.BlockSpec((tm, tn), lambda i, j: (i, j)),
        ],
        out_specs=pl.BlockSpec((tm, tn), lambda i, j: (i, j)),
        out_shape=jax.ShapeDtypeStruct((n, d), F32),
        compiler_params=_cparams(("parallel", "parallel")),
        name=name,
    )(a, w, x)


def _merge_kernel(oa_ref, ob_ref, oc_ref, wa_ref, wb_ref, wc_ref, ga_ref, gb_ref, gc_ref, o_ref):
    merged = ga_ref[...].astype(F32) * _dot(oa_ref[...], wa_ref[...])
    merged += gb_ref[...].astype(F32) * _dot(ob_ref[...], wb_ref[...])
    merged += gc_ref[...].astype(F32) * _dot(oc_ref[...], wc_ref[...])
    o_ref[...] = merged.astype(BF16)


def merge_branches(o_a, o_b, o_c, wa, wb, wc, gates, tm=1024, tn=1024):
    n = o_a.shape[0]
    assert D_MODEL % tn == 0
    per_branch = D_MODEL // tn

    def gate_spec(br):
        return pl.BlockSpec((tm, tn), lambda i, j: (i, br * per_branch + j))

    return pl.pallas_call(
        _merge_kernel,
        grid=(n // tm, D_MODEL // tn),
        in_specs=[
            pl.BlockSpec((tm, A_WIDTH), lambda i, j: (i, 0)),
            pl.BlockSpec((tm, B_WIDTH), lambda i, j: (i, 0)),
            pl.BlockSpec((tm, C_WIDTH), lambda i, j: (i, 0)),
            pl.BlockSpec((A_WIDTH, tn), lambda i, j: (0, j)),
            pl.BlockSpec((B_WIDTH, tn), lambda i, j: (0, j)),
            pl.BlockSpec((C_WIDTH, tn), lambda i, j: (0, j)),
            gate_spec(0), gate_spec(1), gate_spec(2),
        ],
        out_specs=pl.BlockSpec((tm, tn), lambda i, j: (i, j)),
        out_shape=jax.ShapeDtypeStruct((n, D_MODEL), BF16),
        compiler_params=_cparams(("parallel", "parallel")),
        name="merge_branches",
    )(o_a, o_b, o_c, wa, wb, wc, gates, gates, gates)


def _rope_partial(x, cos, sin):
    half = ROPE_DIM // 2
    lane = lax.broadcasted_iota(jnp.int32, x.shape, 1)
    partner = jnp.where(lane < half, pltpu.roll(x, HEAD_DIM - half, 1), pltpu.roll(x, half, 1))
    return x * cos + partner * sin


def _rope_full(x, cos, sin):
    return x * cos + pltpu.roll(x, HEAD_DIM // 2, 1) * sin


def _rotary_tables(t, inv_freq, rot):
    pos = jnp.arange(t, dtype=F32)
    ang = pos[:, None] * inv_freq[None, :]
    ang = jnp.concatenate([ang, ang], axis=-1)
    cos, sin = jnp.cos(ang), jnp.sin(ang)
    sign = jnp.concatenate([-jnp.ones((rot // 2,), F32), jnp.ones((rot // 2,), F32)])
    pad = HEAD_DIM - rot
    cos = jnp.concatenate([cos, jnp.ones((t, pad), F32)], axis=-1)
    sin = jnp.concatenate([sin * sign[None, :], jnp.zeros((t, pad), F32)], axis=-1)
    return cos, sin


def _moba_kprep_kernel(k_ref, v_ref, g_ref, cos_ref, sin_ref, k3_ref, v3_ref, kmean_ref):
    nb = k3_ref.shape[0]
    for j in range(nb):
        rows = slice(j * MOBA_BLOCK, (j + 1) * MOBA_BLOCK)
        x = k_ref[rows, :]
        ms = jnp.mean(x * x, axis=-1, keepdims=True)
        kn = x * lax.rsqrt(ms + NORM_EPS) * g_ref[...]
        kr = _rope_partial(kn, cos_ref[rows, :], sin_ref[rows, :])
        k3_ref[j] = kr.astype(BF16)
        kmean_ref[0, j:j + 1, :] = jnp.mean(kr, axis=0, keepdims=True)
        v3_ref[j] = v_ref[rows, :].astype(BF16)


def moba_kprep(proj, k_gain, cos, sin, b, t):
    nb = t // MOBA_BLOCK
    bh = b * A_HEADS
    return pl.pallas_call(
        _moba_kprep_kernel,
        grid=(b, A_HEADS),
        in_specs=[
            pl.BlockSpec((t, HEAD_DIM), lambda i, h: (i, BLK_KA + h)),
            pl.BlockSpec((t, HEAD_DIM), lambda i, h: (i, BLK_VA + h)),
            pl.BlockSpec((1, HEAD_DIM), lambda i, h: (0, 0)),
            pl.BlockSpec((t, HEAD_DIM), lambda i, h: (0, 0)),
            pl.BlockSpec((t, HEAD_DIM), lambda i, h: (0, 0)),
        ],
        out_specs=[
            pl.BlockSpec((nb, MOBA_BLOCK, HEAD_DIM), lambda i, h: (i * A_HEADS + h, 0, 0)),
            pl.BlockSpec((nb, MOBA_BLOCK, HEAD_DIM), lambda i, h: (i * A_HEADS + h, 0, 0)),
            pl.BlockSpec((1, nb, HEAD_DIM), lambda i, h: (i * A_HEADS + h, 0, 0)),
        ],
        out_shape=[
            jax.ShapeDtypeStruct((bh * nb, MOBA_BLOCK, HEAD_DIM), BF16),
            jax.ShapeDtypeStruct((bh * nb, MOBA_BLOCK, HEAD_DIM), BF16),
            jax.ShapeDtypeStruct((bh, nb, HEAD_DIM), F32),
        ],
        compiler_params=_cparams(("parallel", "parallel")),
        name="moba_kprep",
    )(proj, proj, k_gain, cos, sin)


def _moba_attn_kernel(q_ref, g_ref, cos_ref, sin_ref, k3_ref, v3_ref, kmean_ref, o_ref,
                      qa_ref, m_ref, l_ref, acc_ref):
    nb = kmean_ref.shape[1]
    qi = pl.program_id(1)
    d = HEAD_DIM
    heads = range(A_HEADS)
    blk_rows = MOBA_BLOCK
    ones_cols = jnp.ones((blk_rows, d), BF16)
    lane = lax.broadcasted_iota(jnp.int32, (blk_rows, d), 1)

    def probs(h, s, first):
        m_blk = jnp.max(s, axis=-1, keepdims=True)
        if first:
            m_new, alpha = jnp.broadcast_to(m_blk, (blk_rows, d)), None
        else:
            m_old = m_ref[h]
            m_new = jnp.maximum(m_old, m_blk)
            alpha = jnp.exp2(m_old - m_new)
        m_ref[h] = m_new
        p = jnp.concatenate([jnp.exp2(s[:, :d] - m_new), jnp.exp2(s[:, d:] - m_new)], axis=1)
        return p.astype(BF16), alpha

    def accumulate(h, p, alpha, blk_idx):
        pv = _dot(p, jnp.concatenate([v3_ref[blk_idx], ones_cols], axis=1))
        if alpha is None:
            l_ref[h] = pv[:, d:]
            acc_ref[h] = pv[:, :d]
        else:
            l_ref[h] = alpha * l_ref[h] + pv[:, d:]
            acc_ref[h] = alpha * acc_ref[h] + pv[:, :d]

    def pipelined(items, score, first):
        ahead = 3
        s = {i: score(*items[i]) for i in range(min(ahead, len(items)))}
        for i, (h, j, _) in enumerate(items):
            p, alpha = probs(h, s.pop(i), first)
            if i + ahead < len(items):
                s[i + ahead] = score(*items[i + ahead])
            accumulate(h, p, alpha, h * nb + j)

    qr = []
    for h in heads:
        x = q_ref[:, h * d:(h + 1) * d]
        ms = jnp.mean(x * x, axis=-1, keepdims=True)
        qn = x * lax.rsqrt(ms + NORM_EPS) * g_ref[...]
        qr.append(_rope_partial(qn, cos_ref[...], sin_ref[...]))
    gates = [_dot_nt(kmean_ref[h], qr[h], precision=lax.Precision.HIGHEST) for h in heads]
    blk = lax.broadcasted_iota(jnp.int32, gates[0].shape, 0)
    past = blk < qi
    for h in heads:
        gate = jnp.where(past, gates[h], -jnp.inf)
        rank = jnp.zeros(gate.shape, F32)
        for i in range(nb):
            gi = gate[i:i + 1, :]
            wins = jnp.where(gi > gate, 1.0, jnp.where(gi == gate, jnp.where(blk > i, 1.0, 0.0), 0.0))
            rank = rank + wins
        bias = jnp.where(past, jnp.where(rank < MOBA_TOPK, 0.0, NEG_BIG), NEG_BIG)
        bias_q = jnp.concatenate([bias, jnp.zeros((d - nb, blk_rows), F32)], axis=0).T
        qs = qr[h] * (d ** -0.5 * LOG2_E)
        qa_ref[h] = jnp.concatenate([qs.astype(BF16), bias_q.astype(BF16)], axis=1)

    q_pos = lax.broadcasted_iota(jnp.int32, (blk_rows, blk_rows), 0)
    key_pos = lax.broadcasted_iota(jnp.int32, (blk_rows, blk_rows), 1)
    causal = key_pos <= q_pos

    def own_score(h, j, _):
        return jnp.where(causal, _dot_nt(qa_ref[h][:, :d], k3_ref[h * nb + j]), NEG_BIG)

    pipelined([(h, qi, None) for h in heads], own_score, True)

    def past_blocks(js):
        def past_score(h, j, onehot):
            return _dot_nt(qa_ref[h], jnp.concatenate([k3_ref[h * nb + j], onehot], axis=1))

        onehots = [jnp.where(lane == j, 1.0, 0.0).astype(BF16) for j in js]
        pipelined([(h, j, oh) for j, oh in zip(js, onehots) for h in heads], past_score, False)

    def body(jj, carry):
        past_blocks([2 * jj, 2 * jj + 1])
        return carry

    lax.fori_loop(0, qi // 2, body, 0)

    @pl.when(qi % 2 == 1)
    def _():
        past_blocks([qi - 1])
    for h in heads:
        o_ref[:, h * d:(h + 1) * d] = (acc_ref[h] / l_ref[h]).astype(BF16)


def moba_attention(proj, q_gain, cos, sin, k3, v3, kmean, b, t):
    nb = t // MOBA_BLOCK
    n = b * t
    hb = A_HEADS * nb
    stat = pltpu.VMEM((A_HEADS, MOBA_BLOCK, HEAD_DIM), F32)
    return pl.pallas_call(
        _moba_attn_kernel,
        grid=(b, nb),
        in_specs=[
            pl.BlockSpec((MOBA_BLOCK, A_WIDTH), lambda i, q: (i * nb + q, BLK_QA // A_HEADS)),
            pl.BlockSpec((1, HEAD_DIM), lambda i, q: (0, 0)),
            pl.BlockSpec((MOBA_BLOCK, HEAD_DIM), lambda i, q: (q, 0)),
            pl.BlockSpec((MOBA_BLOCK, HEAD_DIM), lambda i, q: (q, 0)),
            pl.BlockSpec((hb, MOBA_BLOCK, HEAD_DIM), lambda i, q: (i, 0, 0)),
            pl.BlockSpec((hb, MOBA_BLOCK, HEAD_DIM), lambda i, q: (i, 0, 0)),
            pl.BlockSpec((A_HEADS, nb, HEAD_DIM), lambda i, q: (i, 0, 0)),
        ],
        out_specs=pl.BlockSpec((MOBA_BLOCK, A_WIDTH), lambda i, q: (i * nb + q, 0)),
        out_shape=jax.ShapeDtypeStruct((n, A_WIDTH), BF16),
        scratch_shapes=[pltpu.VMEM((A_HEADS, MOBA_BLOCK, 2 * HEAD_DIM), BF16), stat, stat, stat],
        compiler_params=_cparams(("parallel", "arbitrary")),
        name="moba_attention",
    )(proj, q_gain, cos, sin, k3, v3, kmean)


def _retention_kernel(q_ref, k_ref, v_ref, gate_ref, cos_ref, sin_ref, decay_ref, zeta_ref, xi_ref,
                      gamma_ref, gain_ref, o_ref, state_ref):
    @pl.when(pl.program_id(1) == 0)
    def _():
        state_ref[...] = jnp.zeros_like(state_ref)

    c = RET_CHUNK
    d = HEAD_DIM
    heads = range(B_HEADS)
    hcols = [slice(h * d, (h + 1) * d) for h in heads]
    state = [state_ref[h] for h in heads]
    for n in range(q_ref.shape[0] // c):
        rows = slice(n * c, (n + 1) * c)
        cos, sin = cos_ref[rows, :], sin_ref[rows, :]
        k = [_rope_full(k_ref[rows, hcols[h]], cos, sin) * (d ** -0.5) for h in heads]
        qb = [_rope_full(q_ref[rows, hcols[h]], cos, sin).astype(BF16) for h in heads]
        kb = [k[h].astype(BF16) for h in heads]
        vb = [v_ref[rows, hcols[h]].astype(BF16) for h in heads]
        scores = [_dot_nt(qb[h], kb[h]) for h in heads]
        cross = [_dot(qb[h], state[h].astype(BF16)) for h in heads]
        kz_t = [(k[h] * zeta_ref[h]).T.astype(BF16) for h in heads]
        inner = [_dot((scores[h] * decay_ref[h]).astype(BF16), vb[h]) for h in heads]
        update = [_dot(kz_t[h], vb[h]) for h in heads]
        for h in heads:
            state[h] = state[h] * gamma_ref[h, 0:1, :] + update[h]
            ret = inner[h] + cross[h] * xi_ref[h]
            ms = jnp.mean(ret * ret, axis=-1, keepdims=True)
            y = ret * lax.rsqrt(ms + NORM_EPS) * gain_ref[h]
            o_ref[rows, hcols[h]] = (y * _silu(gate_ref[rows, hcols[h]])).astype(BF16)
    for h in heads:
        state_ref[h] = state[h]


def retention(proj, cos, sin, decay, zeta, xi, gamma, gain, b, t, rows=512):
    n = b * t
    steps = t // rows
    b_blk = BLK_QB // B_HEADS

    def cur(off):
        return pl.BlockSpec((rows, B_WIDTH), lambda i, s: (i * steps + s, b_blk + off))

    def table(shape):
        return pl.BlockSpec((B_HEADS,) + shape, lambda i, s: (0, 0, 0))

    return pl.pallas_call(
        _retention_kernel,
        grid=(b, steps),
        in_specs=[
            cur(0), cur(1), cur(2), cur(3),
            pl.BlockSpec((rows, HEAD_DIM), lambda i, s: (s, 0)),
            pl.BlockSpec((rows, HEAD_DIM), lambda i, s: (s, 0)),
            table((RET_CHUNK, RET_CHUNK)), table((RET_CHUNK, HEAD_DIM)), table((RET_CHUNK, HEAD_DIM)),
            table((8, HEAD_DIM)), table((1, HEAD_DIM)),
        ],
        out_specs=pl.BlockSpec((rows, B_WIDTH), lambda i, s: (i * steps + s, 0)),
        out_shape=jax.ShapeDtypeStruct((n, B_WIDTH), BF16),
        scratch_shapes=[pltpu.VMEM((B_HEADS, HEAD_DIM, HEAD_DIM), F32)],
        compiler_params=_cparams(("parallel", "arbitrary")),
        name="retention",
    )(proj, proj, proj, proj, cos, sin, decay, zeta, xi, gamma, gain)


def _retention_tables():
    c = RET_CHUNK
    log_gamma = jnp.log1p(-jnp.exp2(-5.0 - jnp.arange(B_HEADS, dtype=F32)))
    i = jnp.arange(c, dtype=F32)
    lg = log_gamma[:, None]
    tril = jnp.tril(jnp.ones((c, c), dtype=bool))
    rel = i[:, None] - i[None, :]
    decay = jnp.where(tril, jnp.exp(jnp.where(tril, lg[:, :, None] * rel, 0.0)), 0.0)
    zeta = jnp.exp(lg * (c - 1 - i))
    xi = jnp.exp(lg * (i + 1.0))
    gamma_c = jnp.exp(log_gamma * c)
    bcast = lambda v: jnp.broadcast_to(v[:, :, None], (B_HEADS, c, HEAD_DIM))
    gamma = jnp.broadcast_to(gamma_c[:, None, None], (B_HEADS, 8, HEAD_DIM))
    return decay, bcast(zeta), bcast(xi), gamma


def _lane_column(x, idx):
    lane = lax.broadcasted_iota(jnp.int32, x.shape, 1)
    col = jnp.sum(jnp.where(lane == idx, x, 0.0), axis=-1, keepdims=True)
    return jnp.broadcast_to(col, x.shape)


def _gdn_kernel(q_ref, k_ref, v_ref, qp_ref, kp_ref, vp_ref, z_ref, ba_ref, cw_ref, alog_ref, dtb_ref,
                gain_ref, o_ref, state_ref):
    r = GDN_GROUP
    c = GDN_CHUNK
    d = HEAD_DIM
    w = C_WIDTH
    hi = lax.Precision.HIGHEST
    first = pl.program_id(1) == 0

    @pl.when(first)
    def _():
        state_ref[...] = jnp.zeros_like(state_ref)

    row = lax.broadcasted_iota(jnp.int32, (r, r), 0)
    col = lax.broadcasted_iota(jnp.int32, (r, r), 1)
    same_chunk = (row // c) == (col // c)
    tril = same_chunk & (row >= col)
    strict = same_chunk & (row > col)
    lane_r = lax.broadcasted_iota(jnp.int32, (d, r), 1)

    def conv_silu(x_ref, prev_ref, w0):
        prev = jnp.where(first, 0.0, prev_ref[...])
        xe = jnp.concatenate([prev, x_ref[...]], axis=0)
        y = xe[8:, :] * cw_ref[CONV_WIDTH - 1:CONV_WIDTH, w0:w0 + w]
        for back in range(1, CONV_WIDTH):
            tap = CONV_WIDTH - 1 - back
            y = y + pltpu.roll(xe, back, 0)[8:, :] * cw_ref[tap:tap + 1, w0:w0 + w]
        return _silu(y)

    xq = conv_silu(q_ref, qp_ref, 0)
    xk = conv_silu(k_ref, kp_ref, w)
    xv = conv_silu(v_ref, vp_ref, 2 * w)

    ba = ba_ref[...]
    beta_all = _sigmoid(ba)
    pre = ba + dtb_ref[...]
    softplus = jnp.maximum(pre, 0.0) + jnp.log(1.0 + jnp.exp(-jnp.abs(pre)))
    g_all = -jnp.exp(alog_ref[...]) * softplus
    gcum_all = _dot(jnp.where(tril, 1.0, 0.0), g_all, precision=hi)
    gtot_all = _dot(jnp.where(same_chunk, 1.0, 0.0), g_all, precision=hi)

    heads = range(C_HEADS)
    hcols = [slice(h * d, (h + 1) * d) for h in heads]
    q, k, k_beta, kb16, lmask, exp_g, gcum, gtot, x = ([None] * C_HEADS for _ in range(9))
    for h in heads:
        xqh, xkh, v = xq[:, hcols[h]], xk[:, hcols[h]], xv[:, hcols[h]]
        q[h] = xqh * lax.rsqrt(jnp.sum(xqh * xqh, axis=-1, keepdims=True) + NORM_EPS) * (d ** -0.5)
        k[h] = xkh * lax.rsqrt(jnp.sum(xkh * xkh, axis=-1, keepdims=True) + NORM_EPS)
        beta = _lane_column(beta_all, h)
        gcum[h] = _lane_column(gcum_all, C_HEADS + h)
        gtot[h] = _lane_column(gtot_all, C_HEADS + h)
        gcum_col = jnp.concatenate([gcum[h], gcum[h]], axis=1)
        diff = gcum_col - gcum_col.T
        lmask[h] = jnp.where(tril, jnp.exp(jnp.where(tril, diff, 0.0)), 0.0)
        exp_g[h] = jnp.exp(gcum[h])
        k_beta[h] = k[h] * beta
        kb16[h] = k[h].astype(BF16)
        x[h] = jnp.concatenate([v * beta, k_beta[h] * exp_g[h]], axis=1)

    kk = [_dot_nt(k_beta[h].astype(BF16), kb16[h]) for h in heads]
    qk = [_dot_nt(q[h].astype(BF16), kb16[h]) for h in heads]
    p = [-jnp.where(strict, kk[h] * lmask[h], 0.0) for h in heads]
    attn = [(qk[h] * lmask[h]).astype(BF16) for h in heads]

    steps = int(np.log2(c))
    for s in range(steps):
        p16 = [p[h].astype(BF16) for h in heads]
        x = [x[h] + _dot(p16[h], x[h].astype(BF16)) for h in heads]
        if s + 1 < steps:
            p = [_dot(p16[h], p16[h]) for h in heads]

    u = [x[h][:, :d] for h in heads]
    w16 = [x[h][:, d:].astype(BF16) for h in heads]
    qdec16 = [(q[h] * exp_g[h]).astype(BF16) for h in heads]
    kdec_t = [(k[h] * jnp.exp(gtot[h] - gcum[h])).T for h in heads]
    chunk_decay = [jnp.exp(gtot[h]) for h in heads]

    state = [state_ref[h] for h in heads]
    vnew = [[] for _ in heads]
    for n in range(r // c):
        rows = slice(n * c, (n + 1) * c)
        s16 = [state[h].astype(BF16) for h in heads]
        ws = [_dot(w16[h][rows, :], s16[h]) for h in heads]
        qs = [_dot(qdec16[h][rows, :], s16[h]) for h in heads]
        vfull = []
        for h in heads:
            vnew[h].append((u[h][rows, :] - ws[h]).astype(BF16))
            vfull.append(jnp.concatenate(vnew[h] + [jnp.zeros((c, d), BF16)] * (r // c - 1 - n), axis=0))
        av = [_dot(attn[h][rows, :], vfull[h]) for h in heads]
        kd = [jnp.where((lane_r // c) == n, kdec_t[h], 0.0).astype(BF16) for h in heads]
        kv = [_dot(kd[h], vfull[h]) for h in heads]
        for h in heads:
            state[h] = state[h] * chunk_decay[h][n * c:n * c + 1, :] + kv[h]
            o = qs[h] + av[h]
            ms = jnp.mean(o * o, axis=-1, keepdims=True)
            y = o * lax.rsqrt(ms + NORM_EPS) * gain_ref[...]
            o_ref[rows, hcols[h]] = (y * _silu(z_ref[rows, hcols[h]])).astype(BF16)
    for h in heads:
        state_ref[h] = state[h]


def gated_deltanet(proj, conv_w, alog, dtb, gain, b, t):
    n = b * t
    r = GDN_GROUP
    groups = t // r
    c_blk = BLK_QC // C_HEADS

    def cur(off):
        return pl.BlockSpec((r, C_WIDTH), lambda i, g: (i * groups + g, c_blk + off))

    def prev(off):
        return pl.BlockSpec((8, C_WIDTH), lambda i, g: (jnp.maximum((i * groups + g) * (r // 8) - 1, 0), c_blk + off))

    vec = pl.BlockSpec((1, HEAD_DIM), lambda i, g: (0, 0))
    return pl.pallas_call(
        _gdn_kernel,
        grid=(b, groups),
        in_specs=[
            cur(0), cur(1), cur(2), prev(0), prev(1), prev(2), cur(3),
            pl.BlockSpec((r, HEAD_DIM), lambda i, g: (i * groups + g, BLK_BA)),
            pl.BlockSpec((CONV_WIDTH, 3 * C_WIDTH), lambda i, g: (0, 0)),
            vec, vec, vec,
        ],
        out_specs=pl.BlockSpec((r, C_WIDTH), lambda i, g: (i * groups + g, 0)),
        out_shape=jax.ShapeDtypeStruct((n, C_WIDTH), BF16),
        scratch_shapes=[pltpu.VMEM((C_HEADS, HEAD_DIM, HEAD_DIM), F32)],
        compiler_params=_cparams(("parallel", "arbitrary")),
        name="gated_deltanet",
    )(proj, proj, proj, proj, proj, proj, proj, proj, conv_w, alog, dtb, gain)


def _pad_lanes(v, fill=0.0):
    return jnp.concatenate([v, jnp.full((HEAD_DIM - v.shape[0],), fill, F32)])[None, :]


def kernel(x, attn_norm, w_in, q_norm, k_norm, ret_norm, conv_w, a_log, dt_bias, gdn_norm,
           w_branch, w_out, ffn_norm, w_gate, w_up, w_down):
    b, t, d = x.shape
    n = b * t
    depth = w_in.shape[0]
    xf = x.reshape(n, d)

    rope_freq = ROPE_THETA ** (-jnp.arange(0, ROPE_DIM, 2, dtype=F32) / ROPE_DIM)
    ret_freq = RET_THETA ** (-jnp.linspace(0.0, 1.0, HEAD_DIM // 2, dtype=F32))
    cos_a, sin_a = _rotary_tables(t, rope_freq, ROPE_DIM)
    cos_b, sin_b = _rotary_tables(t, ret_freq, HEAD_DIM)
    decay, zeta, xi, gamma = _retention_tables()
    w_proj = w_in_prep(w_in)

    for layer in range(depth):
        proj, gates = in_proj(xf, attn_norm[layer][None, :], w_proj, layer)

        k3, vt3, kmean = moba_kprep(proj, k_norm[layer][None, :], cos_a, sin_a, b, t)
        o_a = moba_attention(proj, q_norm[layer][None, :], cos_a, sin_a, k3, vt3,
                             kmean, b, t)

        o_b = retention(proj, cos_b, sin_b, decay, zeta, xi, gamma,
                        ret_norm[layer].reshape(B_HEADS, 1, HEAD_DIM), b, t)

        alog_v = jnp.concatenate([jnp.zeros((C_HEADS,), F32), a_log[layer]])
        dtb_v = jnp.concatenate([jnp.zeros((C_HEADS,), F32), dt_bias[layer]])
        o_c = gated_deltanet(proj, conv_w[layer], _pad_lanes(alog_v), _pad_lanes(dtb_v),
                             gdn_norm[layer][None, :], b, t)

        wbr = w_branch[layer].astype(BF16)
        merged = merge_branches(o_a, o_b, o_c, wbr[:A_WIDTH], wbr[A_WIDTH:A_WIDTH + B_WIDTH],
                                wbr[A_WIDTH + B_WIDTH:], gates)
        xf = residual_matmul(merged, w_out[layer].astype(BF16), xf, 1024, 1024, "out_proj")

        hidden = ffn_up(xf, ffn_norm[layer][None, :], w_gate[layer].astype(BF16), w_up[layer].astype(BF16))
        xf = residual_matmul(hidden, w_down[layer].astype(BF16), xf, 1024, 512, "ffn_down")
    return xf.reshape(b, t, d)
```

```python
import functools

import jax
import jax.numpy as jnp
import numpy as np
from jax import lax
from jax.experimental import pallas as pl
from jax.experimental.pallas import tpu as pltpu

F32 = jnp.float32
BF16 = jnp.bfloat16

D_MODEL = 2048
HEAD_DIM = 128
A_HEADS = 6
B_HEADS = 5
C_HEADS = 5
A_WIDTH = A_HEADS * HEAD_DIM
B_WIDTH = B_HEADS * HEAD_DIM
C_WIDTH = C_HEADS * HEAD_DIM
N_BRANCHES = 3
MOBA_BLOCK = 256
MOBA_TOPK = 3
ROPE_THETA = 500000.0
ROPE_DIM = HEAD_DIM // 4
RET_THETA = 10000.0
RET_CHUNK = 128
GDN_CHUNK = 64
CONV_WIDTH = 4
NORM_EPS = 1e-6

BLK_QA, BLK_KA, BLK_VA = 0, 6, 12
BLK_BA = 18
BLK_QC, BLK_KC, BLK_VC, BLK_ZC = 20, 25, 30, 35
BLK_QB, BLK_KB, BLK_VB, BLK_GB = 40, 45, 50, 55
BLK_GATE = 60
N_BA = 2 * C_HEADS
GATE_COL0 = BLK_GATE * HEAD_DIM
PROJ_BLOCKS = BLK_GATE + N_BRANCHES * D_MODEL // HEAD_DIM
SRC_BLK_B, SRC_BLK_C, SRC_BLK_BA = 18, 38, 58

GDN_GROUP = 4 * GDN_CHUNK
NEG_BIG = -1e30
LOG2_E = 1.4426950408889634
VMEM_LIMIT = 52 * 1024 * 1024


def _cparams(sem):
    return pltpu.CompilerParams(dimension_semantics=sem, vmem_limit_bytes=VMEM_LIMIT)


def _sigmoid(x):
    return 1.0 / (1.0 + jnp.exp(-x))


def _silu(x):
    return x * _sigmoid(x)


def _dot(a, b, precision=None):
    return jnp.dot(a, b, preferred_element_type=F32, precision=precision)


def _dot_nt(a, b, precision=None):
    return lax.dot_general(a, b, (((1,), (1,)), ((), ())), preferred_element_type=F32, precision=precision)


def _w_in_src_col(j):
    hd = HEAD_DIM
    return jnp.where(
        j < BLK_BA, j * hd,
        jnp.where(j < BLK_QC, SRC_BLK_BA * hd,
                  jnp.where(j < BLK_QB, (j - BLK_QC + SRC_BLK_C) * hd,
                            jnp.where(j < BLK_GATE, (j - BLK_QB + SRC_BLK_B) * hd,
                                      (j - BLK_GATE + SRC_BLK_BA) * hd + N_BA))))


def _w_in_prep_kernel(a_ref, o_ref):
    j = pl.program_id(0)
    depth = o_ref.shape[0]
    special = (j == BLK_BA) | (j == BLK_BA + 1)

    @pl.when(jnp.logical_not(special))
    def _():
        for l in range(depth):
            o_ref[l] = a_ref[:, l, :].T.astype(BF16)

    @pl.when(special)
    def _():
        row = lax.broadcasted_iota(jnp.int32, (a_ref.shape[0], a_ref.shape[2]), 0)
        keep = (row < N_BA) & (j == BLK_BA)
        for l in range(depth):
            o_ref[l] = jnp.where(keep, a_ref[:, l, :], 0.0).T.astype(BF16)


def w_in_prep(w_in):
    depth, d, _ = w_in.shape
    return pl.pallas_call(
        _w_in_prep_kernel,
        grid=(PROJ_BLOCKS,),
        in_specs=[pl.BlockSpec((pl.Element(HEAD_DIM), pl.Element(depth), pl.Element(d)),
                               lambda j: (_w_in_src_col(j), 0, 0))],
        out_specs=pl.BlockSpec((depth, d, HEAD_DIM), lambda j: (0, 0, j)),
        out_shape=jax.ShapeDtypeStruct((depth, d, PROJ_BLOCKS * HEAD_DIM), BF16),
        compiler_params=_cparams(("parallel",)),
        name="w_in_prep",
    )(jnp.transpose(w_in, (2, 0, 1)))


def _norm_rows_to(h_ref, x_ref, g_ref, rows_per_step=256):
    n = x_ref.shape[0] // rows_per_step

    def body(r, c):
        rows = pl.ds(pl.multiple_of(r * rows_per_step, rows_per_step), rows_per_step)
        x = x_ref[rows, :]
        ms = jnp.mean(x * x, axis=-1, keepdims=True)
        h_ref[rows, :] = (x * lax.rsqrt(ms + NORM_EPS) * g_ref[...]).astype(BF16)
        return c

    lax.fori_loop(0, n, body, 0)


def _in_proj_kernel(x_ref, g_ref, w_ref, o_ref, h_ref, *, gates):
    @pl.when(pl.program_id(1) == 0)
    def _():
        _norm_rows_to(h_ref, x_ref, g_ref)

    y = _dot(h_ref[...], w_ref[0])
    o_ref[...] = _sigmoid(y).astype(o_ref.dtype) if gates else y


def in_proj(x, gain, w, layer, gates, tm=1024, tn=1536):
    n, d = x.shape
    col0, cols, dtype = (GATE_COL0, w.shape[2] - GATE_COL0, BF16) if gates else (0, GATE_COL0, F32)
    assert col0 % tn == 0 and cols % tn == 0
    return pl.pallas_call(
        functools.partial(_in_proj_kernel, gates=gates),
        grid=(n // tm, cols // tn),
        in_specs=[
            pl.BlockSpec((tm, d), lambda i, j: (i, 0)),
            pl.BlockSpec((1, d), lambda i, j: (0, 0)),
            pl.BlockSpec((1, d, tn), lambda i, j: (layer, 0, col0 // tn + j)),
        ],
        out_specs=pl.BlockSpec((tm, tn), lambda i, j: (i, j)),
        out_shape=jax.ShapeDtypeStruct((n, cols), dtype),
        scratch_shapes=[pltpu.VMEM((tm, d), BF16)],
        compiler_params=_cparams(("parallel", "arbitrary")),
        name="in_proj_gates" if gates else "in_proj",
    )(x, gain, w)


def _ffn_up_kernel(x_ref, g_ref, wg_ref, wu_ref, o_ref, h_ref):
    @pl.when(pl.program_id(1) == 0)
    def _():
        _norm_rows_to(h_ref, x_ref, g_ref)

    h = h_ref[...]
    o_ref[...] = (_silu(_dot(h, wg_ref[...])) * _dot(h, wu_ref[...])).astype(BF16)


def ffn_up(x, gain, wg, wu, tm=1024, tn=512):
    n, d = x.shape
    cols = wg.shape[1]
    return pl.pallas_call(
        _ffn_up_kernel,
        grid=(n // tm, cols // tn),
        in_specs=[
            pl.BlockSpec((tm, d), lambda i, j: (i, 0)),
            pl.BlockSpec((1, d), lambda i, j: (0, 0)),
            pl.BlockSpec((d, tn), lambda i, j: (0, j)),
            pl.BlockSpec((d, tn), lambda i, j: (0, j)),
        ],
        out_specs=pl.BlockSpec((tm, tn), lambda i, j: (i, j)),
        out_shape=jax.ShapeDtypeStruct((n, cols), BF16),
        scratch_shapes=[pltpu.VMEM((tm, d), BF16)],
        compiler_params=_cparams(("parallel", "arbitrary")),
        name="ffn_up",
    )(x, gain, wg, wu)


def _residual_matmul_kernel(a_ref, w_ref, x_ref, o_ref):
    o_ref[...] = x_ref[...] + _dot(a_ref[...], w_ref[...])


def residual_matmul(a, w, x, tm, tn, name):
    n, k = a.shape
    d = w.shape[1]
    return pl.pallas_call(
        _residual_matmul_kernel,
        grid=(n // tm, d // tn),
        in_specs=[
            pl.BlockSpec((tm, k), lambda i, j: (i, 0)),
            pl.BlockSpec((k, tn), lambda i, j: (0, j)),
            pl.BlockSpec((tm, tn), lambda i, j: (i, j)),
        ],
        out_specs=pl.BlockSpec((tm, tn), lambda i, j: (i, j)),
        out_shape=jax.ShapeDtypeStruct((n, d), F32),
        compiler_params=_cparams(("parallel", "parallel")),
        name=name,
    )(a, w, x)


def _merge_kernel(oa_ref, ob_ref, oc_ref, wa_ref, wb_ref, wc_ref, ga_ref, gb_ref, gc_ref, o_ref):
    merged = ga_ref[...].astype(F32) * _dot(oa_ref[...], wa_ref[...])
    merged += gb_ref[...].astype(F32) * _dot(ob_ref[...], wb_ref[...])
    merged += gc_ref[...].astype(F32) * _dot(oc_ref[...], wc_ref[...])
    o_ref[...] = merged.astype(BF16)


def merge_branches(o_a, o_b, o_c, wa, wb, wc, gates, tm=1024, tn=1024):
    n = o_a.shape[0]
    assert D_MODEL % tn == 0
    per_branch = D_MODEL // tn

    def gate_spec(br):
        return pl.BlockSpec((tm, tn), lambda i, j: (i, br * per_branch + j))

    return pl.pallas_call(
        _merge_kernel,
        grid=(n // tm, D_MODEL // tn),
        in_specs=[
            pl.BlockSpec((tm, A_WIDTH), lambda i, j: (i, 0)),
            pl.BlockSpec((tm, B_WIDTH), lambda i, j: (i, 0)),
            pl.BlockSpec((tm, C_WIDTH), lambda i, j: (i, 0)),
            pl.BlockSpec((A_WIDTH, tn), lambda i, j: (0, j)),
            pl.BlockSpec((B_WIDTH, tn), lambda i, j: (0, j)),
            pl.BlockSpec((C_WIDTH, tn), lambda i, j: (0, j)),
            gate_spec(0), gate_spec(1), gate_spec(2),
        ],
        out_specs=pl.BlockSpec((tm, tn), lambda i, j: (i, j)),
        out_shape=jax.ShapeDtypeStruct((n, D_MODEL), BF16),
        compiler_params=_cparams(("parallel", "parallel")),
        name="merge_branches",
    )(o_a, o_b, o_c, wa, wb, wc, gates, gates, gates)


def _rope_partial(x, cos, sin):
    half = ROPE_DIM // 2
    lane = lax.broadcasted_iota(jnp.int32, x.shape, 1)
    partner = jnp.where(lane < half, pltpu.roll(x, HEAD_DIM - half, 1), pltpu.roll(x, half, 1))
    return x * cos + partner * sin


def _rope_full(x, cos, sin):
    return x * cos + pltpu.roll(x, HEAD_DIM // 2, 1) * sin


def _rotary_tables(t, inv_freq, rot):
    pos = jnp.arange(t, dtype=F32)
    ang = pos[:, None] * inv_freq[None, :]
    ang = jnp.concatenate([ang, ang], axis=-1)
    cos, sin = jnp.cos(ang), jnp.sin(ang)
    sign = jnp.concatenate([-jnp.ones((rot // 2,), F32), jnp.ones((rot // 2,), F32)])
    pad = HEAD_DIM - rot
    cos = jnp.concatenate([cos, jnp.ones((t, pad), F32)], axis=-1)
    sin = jnp.concatenate([sin * sign[None, :], jnp.zeros((t, pad), F32)], axis=-1)
    return cos, sin


def _moba_kprep_kernel(k_ref, v_ref, g_ref, cos_ref, sin_ref, k3_ref, v3_ref, kmean_ref):
    nb = k3_ref.shape[0]
    for j in range(nb):
        rows = slice(j * MOBA_BLOCK, (j + 1) * MOBA_BLOCK)
        x = k_ref[rows, :]
        ms = jnp.mean(x * x, axis=-1, keepdims=True)
        kn = x * lax.rsqrt(ms + NORM_EPS) * g_ref[...]
        kr = _rope_partial(kn, cos_ref[rows, :], sin_ref[rows, :])
        k3_ref[j] = kr.astype(BF16)
        kmean_ref[0, j:j + 1, :] = jnp.mean(kr, axis=0, keepdims=True)
        v3_ref[j] = v_ref[rows, :].astype(BF16)


def moba_kprep(proj, k_gain, cos, sin, b, t):
    nb = t // MOBA_BLOCK
    bh = b * A_HEADS
    return pl.pallas_call(
        _moba_kprep_kernel,
        grid=(b, A_HEADS),
        in_specs=[
            pl.BlockSpec((t, HEAD_DIM), lambda i, h: (i, BLK_KA + h)),
            pl.BlockSpec((t, HEAD_DIM), lambda i, h: (i, BLK_VA + h)),
            pl.BlockSpec((1, HEAD_DIM), lambda i, h: (0, 0)),
            pl.BlockSpec((t, HEAD_DIM), lambda i, h: (0, 0)),
            pl.BlockSpec((t, HEAD_DIM), lambda i, h: (0, 0)),
        ],
        out_specs=[
            pl.BlockSpec((nb, MOBA_BLOCK, HEAD_DIM), lambda i, h: (i * A_HEADS + h, 0, 0)),
            pl.BlockSpec((nb, MOBA_BLOCK, HEAD_DIM), lambda i, h: (i * A_HEADS + h, 0, 0)),
            pl.BlockSpec((1, nb, HEAD_DIM), lambda i, h: (i * A_HEADS + h, 0, 0)),
        ],
        out_shape=[
            jax.ShapeDtypeStruct((bh * nb, MOBA_BLOCK, HEAD_DIM), BF16),
            jax.ShapeDtypeStruct((bh * nb, MOBA_BLOCK, HEAD_DIM), BF16),
            jax.ShapeDtypeStruct((bh, nb, HEAD_DIM), F32),
        ],
        compiler_params=_cparams(("parallel", "parallel")),
        name="moba_kprep",
    )(proj, proj, k_gain, cos, sin)


def _moba_attn_kernel(q_ref, g_ref, cos_ref, sin_ref, k3_ref, v3_ref, kmean_ref, o_ref,
                      qa_ref, m_ref, l_ref, acc_ref):
    nb = kmean_ref.shape[1]
    qi = pl.program_id(1)
    d = HEAD_DIM
    heads = range(A_HEADS)
    blk_rows = MOBA_BLOCK
    ones_cols = jnp.ones((blk_rows, d), BF16)
    lane = lax.broadcasted_iota(jnp.int32, (blk_rows, d), 1)

    def probs(h, s, first):
        m_blk = jnp.max(s, axis=-1, keepdims=True)
        if first:
            m_new, alpha = jnp.broadcast_to(m_blk, (blk_rows, d)), None
        else:
            m_old = m_ref[h]
            m_new = jnp.maximum(m_old, m_blk)
            alpha = jnp.exp2(m_old - m_new)
        m_ref[h] = m_new
        p = jnp.concatenate([jnp.exp2(s[:, :d] - m_new), jnp.exp2(s[:, d:] - m_new)], axis=1)
        return p.astype(BF16), alpha

    def accumulate(h, p, alpha, blk_idx):
        pv = _dot(p, jnp.concatenate([v3_ref[blk_idx], ones_cols], axis=1))
        if alpha is None:
            l_ref[h] = pv[:, d:]
            acc_ref[h] = pv[:, :d]
        else:
            l_ref[h] = alpha * l_ref[h] + pv[:, d:]
            acc_ref[h] = alpha * acc_ref[h] + pv[:, :d]

    def pipelined(items, score, first):
        ahead = 3 if len(items) <= A_HEADS else 2
        s = {i: score(*items[i]) for i in range(min(ahead, len(items)))}
        for i, (h, j, _) in enumerate(items):
            p, alpha = probs(h, s.pop(i), first)
            if i + ahead < len(items):
                s[i + ahead] = score(*items[i + ahead])
            accumulate(h, p, alpha, h * nb + j)

    qr = []
    for h in heads:
        x = q_ref[:, h * d:(h + 1) * d]
        ms = jnp.mean(x * x, axis=-1, keepdims=True)
        qn = x * lax.rsqrt(ms + NORM_EPS) * g_ref[...]
        qr.append(_rope_partial(qn, cos_ref[...], sin_ref[...]))
    gates = [_dot_nt(kmean_ref[h], qr[h], precision=lax.Precision.HIGHEST) for h in heads]
    blk = lax.broadcasted_iota(jnp.int32, gates[0].shape, 0)
    past = blk < qi
    for h in heads:
        gate = jnp.where(past, gates[h], -jnp.inf)
        rank = jnp.zeros(gate.shape, F32)
        for i in range(nb):
            gi = gate[i:i + 1, :]
            wins = jnp.where(gi > gate, 1.0, jnp.where(gi == gate, jnp.where(blk > i, 1.0, 0.0), 0.0))
            rank = rank + wins
        bias = jnp.where(past, jnp.where(rank < MOBA_TOPK, 0.0, NEG_BIG), NEG_BIG)
        bias_q = jnp.concatenate([bias, jnp.zeros((d - nb, blk_rows), F32)], axis=0).T
        qs = qr[h] * (d ** -0.5 * LOG2_E)
        qa_ref[h] = jnp.concatenate([qs.astype(BF16), bias_q.astype(BF16)], axis=1)

    q_pos = lax.broadcasted_iota(jnp.int32, (blk_rows, blk_rows), 0)
    key_pos = lax.broadcasted_iota(jnp.int32, (blk_rows, blk_rows), 1)
    causal = key_pos <= q_pos

    def own_score(h, j, _):
        return jnp.where(causal, _dot_nt(qa_ref[h][:, :d], k3_ref[h * nb + j]), NEG_BIG)

    pipelined([(h, qi, None) for h in heads], own_score, True)

    def past_blocks(js):
        def past_score(h, j, onehot):
            return _dot_nt(qa_ref[h], jnp.concatenate([k3_ref[h * nb + j], onehot], axis=1))

        onehots = [jnp.where(lane == j, 1.0, 0.0).astype(BF16) for j in js]
        pipelined([(h, j, oh) for j, oh in zip(js, onehots) for h in heads], past_score, False)

    def body(jj, carry):
        past_blocks([2 * jj, 2 * jj + 1])
        return carry

    lax.fori_loop(0, qi // 2, body, 0)

    @pl.when(qi % 2 == 1)
    def _():
        past_blocks([qi - 1])
    for h in heads:
        o_ref[:, h * d:(h + 1) * d] = (acc_ref[h] / l_ref[h]).astype(BF16)


def moba_attention(proj, q_gain, cos, sin, k3, v3, kmean, b, t):
    nb = t // MOBA_BLOCK
    n = b * t
    hb = A_HEADS * nb
    stat = pltpu.VMEM((A_HEADS, MOBA_BLOCK, HEAD_DIM), F32)
    return pl.pallas_call(
        _moba_attn_kernel,
        grid=(b, nb),
        in_specs=[
            pl.BlockSpec((MOBA_BLOCK, A_WIDTH), lambda i, q: (i * nb + q, BLK_QA // A_HEADS)),
            pl.BlockSpec((1, HEAD_DIM), lambda i, q: (0, 0)),
            pl.BlockSpec((MOBA_BLOCK, HEAD_DIM), lambda i, q: (q, 0)),
            pl.BlockSpec((MOBA_BLOCK, HEAD_DIM), lambda i, q: (q, 0)),
            pl.BlockSpec((hb, MOBA_BLOCK, HEAD_DIM), lambda i, q: (i, 0, 0)),
            pl.BlockSpec((hb, MOBA_BLOCK, HEAD_DIM), lambda i, q: (i, 0, 0)),
            pl.BlockSpec((A_HEADS, nb, HEAD_DIM), lambda i, q: (i, 0, 0)),
        ],
        out_specs=pl.BlockSpec((MOBA_BLOCK, A_WIDTH), lambda i, q: (i * nb + q, 0)),
        out_shape=jax.ShapeDtypeStruct((n, A_WIDTH), BF16),
        scratch_shapes=[pltpu.VMEM((A_HEADS, MOBA_BLOCK, 2 * HEAD_DIM), BF16), stat, stat, stat],
        compiler_params=_cparams(("parallel", "arbitrary")),
        name="moba_attention",
    )(proj, q_gain, cos, sin, k3, v3, kmean)


def _retention_kernel(q_ref, k_ref, v_ref, gate_ref, cos_ref, sin_ref, decay_ref, zeta_ref, xi_ref,
                      gamma_ref, gain_ref, o_ref, state_ref):
    @pl.when(pl.program_id(1) == 0)
    def _():
        state_ref[...] = jnp.zeros_like(state_ref)

    c = RET_CHUNK
    d = HEAD_DIM
    heads = range(B_HEADS)
    hcols = [slice(h * d, (h + 1) * d) for h in heads]
    state = [state_ref[h] for h in heads]
    for n in range(q_ref.shape[0] // c):
        rows = slice(n * c, (n + 1) * c)
        cos, sin = cos_ref[rows, :], sin_ref[rows, :]
        k = [_rope_full(k_ref[rows, hcols[h]], cos, sin) * (d ** -0.5) for h in heads]
        qb = [_rope_full(q_ref[rows, hcols[h]], cos, sin).astype(BF16) for h in heads]
        kb = [k[h].astype(BF16) for h in heads]
        vb = [v_ref[rows, hcols[h]].astype(BF16) for h in heads]
        scores = [_dot_nt(qb[h], kb[h]) for h in heads]
        cross = [_dot(qb[h], state[h].astype(BF16)) for h in heads]
        kz_t = [(k[h] * zeta_ref[h]).T.astype(BF16) for h in heads]
        inner = [_dot((scores[h] * decay_ref[h]).astype(BF16), vb[h]) for h in heads]
        update = [_dot(kz_t[h], vb[h]) for h in heads]
        for h in heads:
            state[h] = state[h] * gamma_ref[h, 0:1, :] + update[h]
            ret = inner[h] + cross[h] * xi_ref[h]
            ms = jnp.mean(ret * ret, axis=-1, keepdims=True)
            y = ret * lax.rsqrt(ms + NORM_EPS) * gain_ref[h]
            o_ref[rows, hcols[h]] = (y * _silu(gate_ref[rows, hcols[h]])).astype(BF16)
    for h in heads:
        state_ref[h] = state[h]


def retention(proj, cos, sin, decay, zeta, xi, gamma, gain, b, t, rows=512):
    n = b * t
    steps = t // rows
    b_blk = BLK_QB // B_HEADS

    def cur(off):
        return pl.BlockSpec((rows, B_WIDTH), lambda i, s: (i * steps + s, b_blk + off))

    def table(shape):
        return pl.BlockSpec((B_HEADS,) + shape, lambda i, s: (0, 0, 0))

    return pl.pallas_call(
        _retention_kernel,
        grid=(b, steps),
        in_specs=[
            cur(0), cur(1), cur(2), cur(3),
            pl.BlockSpec((rows, HEAD_DIM), lambda i, s: (s, 0)),
            pl.BlockSpec((rows, HEAD_DIM), lambda i, s: (s, 0)),
            table((RET_CHUNK, RET_CHUNK)), table((RET_CHUNK, HEAD_DIM)), table((RET_CHUNK, HEAD_DIM)),
            table((8, HEAD_DIM)), table((1, HEAD_DIM)),
        ],
        out_specs=pl.BlockSpec((rows, B_WIDTH), lambda i, s: (i * steps + s, 0)),
        out_shape=jax.ShapeDtypeStruct((n, B_WIDTH), BF16),
        scratch_shapes=[pltpu.VMEM((B_HEADS, HEAD_DIM, HEAD_DIM), F32)],
        compiler_params=_cparams(("parallel", "arbitrary")),
        name="retention",
    )(proj, proj, proj, proj, cos, sin, decay, zeta, xi, gamma, gain)


def _retention_tables():
    c = RET_CHUNK
    log_gamma = jnp.log1p(-jnp.exp2(-5.0 - jnp.arange(B_HEADS, dtype=F32)))
    i = jnp.arange(c, dtype=F32)
    lg = log_gamma[:, None]
    tril = jnp.tril(jnp.ones((c, c), dtype=bool))
    rel = i[:, None] - i[None, :]
    decay = jnp.where(tril, jnp.exp(jnp.where(tril, lg[:, :, None] * rel, 0.0)), 0.0)
    zeta = jnp.exp(lg * (c - 1 - i))
    xi = jnp.exp(lg * (i + 1.0))
    gamma_c = jnp.exp(log_gamma * c)
    bcast = lambda v: jnp.broadcast_to(v[:, :, None], (B_HEADS, c, HEAD_DIM))
    gamma = jnp.broadcast_to(gamma_c[:, None, None], (B_HEADS, 8, HEAD_DIM))
    return decay, bcast(zeta), bcast(xi), gamma


def _lane_column(x, idx):
    lane = lax.broadcasted_iota(jnp.int32, x.shape, 1)
    col = jnp.sum(jnp.where(lane == idx, x, 0.0), axis=-1, keepdims=True)
    return jnp.broadcast_to(col, x.shape)


def _gdn_kernel(q_ref, k_ref, v_ref, qp_ref, kp_ref, vp_ref, z_ref, ba_ref, cw_ref, alog_ref, dtb_ref,
                gain_ref, o_ref, state_ref):
    r = GDN_GROUP
    c = GDN_CHUNK
    d = HEAD_DIM
    w = C_WIDTH
    hi = lax.Precision.HIGHEST
    first = pl.program_id(1) == 0

    @pl.when(first)
    def _():
        state_ref[...] = jnp.zeros_like(state_ref)

    row = lax.broadcasted_iota(jnp.int32, (r, r), 0)
    col = lax.broadcasted_iota(jnp.int32, (r, r), 1)
    same_chunk = (row // c) == (col // c)
    tril = same_chunk & (row >= col)
    strict = same_chunk & (row > col)
    lane_r = lax.broadcasted_iota(jnp.int32, (d, r), 1)

    def conv_silu(x_ref, prev_ref, w0):
        prev = jnp.where(first, 0.0, prev_ref[...])
        xe = jnp.concatenate([prev, x_ref[...]], axis=0)
        y = xe[8:, :] * cw_ref[CONV_WIDTH - 1:CONV_WIDTH, w0:w0 + w]
        for back in range(1, CONV_WIDTH):
            tap = CONV_WIDTH - 1 - back
            y = y + pltpu.roll(xe, back, 0)[8:, :] * cw_ref[tap:tap + 1, w0:w0 + w]
        return _silu(y)

    xq = conv_silu(q_ref, qp_ref, 0)
    xk = conv_silu(k_ref, kp_ref, w)
    xv = conv_silu(v_ref, vp_ref, 2 * w)

    ba = ba_ref[...]
    beta_all = _sigmoid(ba)
    pre = ba + dtb_ref[...]
    softplus = jnp.maximum(pre, 0.0) + jnp.log(1.0 + jnp.exp(-jnp.abs(pre)))
    g_all = -jnp.exp(alog_ref[...]) * softplus
    gcum_all = _dot(jnp.where(tril, 1.0, 0.0), g_all, precision=hi)
    gtot_all = _dot(jnp.where(same_chunk, 1.0, 0.0), g_all, precision=hi)

    heads = range(C_HEADS)
    hcols = [slice(h * d, (h + 1) * d) for h in heads]
    q, k, k_beta, kb16, lmask, exp_g, gcum, gtot, x = ([None] * C_HEADS for _ in range(9))
    for h in heads:
        xqh, xkh, v = xq[:, hcols[h]], xk[:, hcols[h]], xv[:, hcols[h]]
        q[h] = xqh * lax.rsqrt(jnp.sum(xqh * xqh, axis=-1, keepdims=True) + NORM_EPS) * (d ** -0.5)
        k[h] = xkh * lax.rsqrt(jnp.sum(xkh * xkh, axis=-1, keepdims=True) + NORM_EPS)
        beta = _lane_column(beta_all, h)
        gcum[h] = _lane_column(gcum_all, C_HEADS + h)
        gtot[h] = _lane_column(gtot_all, C_HEADS + h)
        gcum_col = jnp.concatenate([gcum[h], gcum[h]], axis=1)
        diff = gcum_col - gcum_col.T
        lmask[h] = jnp.where(tril, jnp.exp(jnp.where(tril, diff, 0.0)), 0.0)
        exp_g[h] = jnp.exp(gcum[h])
        k_beta[h] = k[h] * beta
        kb16[h] = k[h].astype(BF16)
        x[h] = jnp.concatenate([v * beta, k_beta[h] * exp_g[h]], axis=1)

    kk = [_dot_nt(k_beta[h].astype(BF16), kb16[h]) for h in heads]
    qk = [_dot_nt(q[h].astype(BF16), kb16[h]) for h in heads]
    p = [-jnp.where(strict, kk[h] * lmask[h], 0.0) for h in heads]
    attn = [(qk[h] * lmask[h]).astype(BF16) for h in heads]

    steps = int(np.log2(c))
    for s in range(steps):
        p16 = [p[h].astype(BF16) for h in heads]
        x = [x[h] + _dot(p16[h], x[h].astype(BF16)) for h in heads]
        if s + 1 < steps:
            p = [_dot(p16[h], p16[h]) for h in heads]

    u = [x[h][:, :d] for h in heads]
    w16 = [x[h][:, d:].astype(BF16) for h in heads]
    qdec16 = [(q[h] * exp_g[h]).astype(BF16) for h in heads]
    kdec_t = [(k[h] * jnp.exp(gtot[h] - gcum[h])).T for h in heads]
    chunk_decay = [jnp.exp(gtot[h]) for h in heads]

    state = [state_ref[h] for h in heads]
    vnew = [[] for _ in heads]
    for n in range(r // c):
        rows = slice(n * c, (n + 1) * c)
        s16 = [state[h].astype(BF16) for h in heads]
        ws = [_dot(w16[h][rows, :], s16[h]) for h in heads]
        qs = [_dot(qdec16[h][rows, :], s16[h]) for h in heads]
        vfull = []
        for h in heads:
            vnew[h].append((u[h][rows, :] - ws[h]).astype(BF16))
            vfull.append(jnp.concatenate(vnew[h] + [jnp.zeros((c, d), BF16)] * (r // c - 1 - n), axis=0))
        av = [_dot(attn[h][rows, :], vfull[h]) for h in heads]
        kd = [jnp.where((lane_r // c) == n, kdec_t[h], 0.0).astype(BF16) for h in heads]
        kv = [_dot(kd[h], vfull[h]) for h in heads]
        for h in heads:
            state[h] = state[h] * chunk_decay[h][n * c:n * c + 1, :] + kv[h]
            o = qs[h] + av[h]
            ms = jnp.mean(o * o, axis=-1, keepdims=True)
            y = o * lax.rsqrt(ms + NORM_EPS) * gain_ref[...]
            o_ref[rows, hcols[h]] = (y * _silu(z_ref[rows, hcols[h]])).astype(BF16)
    for h in heads:
        state_ref[h] = state[h]


def gated_deltanet(proj, conv_w, alog, dtb, gain, b, t):
    n = b * t
    r = GDN_GROUP
    groups = t // r
    c_blk = BLK_QC // C_HEADS

    def cur(off):
        return pl.BlockSpec((r, C_WIDTH), lambda i, g: (i * groups + g, c_blk + off))

    def prev(off):
        return pl.BlockSpec((8, C_WIDTH), lambda i, g: (jnp.maximum((i * groups + g) * (r // 8) - 1, 0), c_blk + off))

    vec = pl.BlockSpec((1, HEAD_DIM), lambda i, g: (0, 0))
    return pl.pallas_call(
        _gdn_kernel,
        grid=(b, groups),
        in_specs=[
            cur(0), cur(1), cur(2), prev(0), prev(1), prev(2), cur(3),
            pl.BlockSpec((r, HEAD_DIM), lambda i, g: (i * groups + g, BLK_BA)),
            pl.BlockSpec((CONV_WIDTH, 3 * C_WIDTH), lambda i, g: (0, 0)),
            vec, vec, vec,
        ],
        out_specs=pl.BlockSpec((r, C_WIDTH), lambda i, g: (i * groups + g, 0)),
        out_shape=jax.ShapeDtypeStruct((n, C_WIDTH), BF16),
        scratch_shapes=[pltpu.VMEM((C_HEADS, HEAD_DIM, HEAD_DIM), F32)],
        compiler_params=_cparams(("parallel", "arbitrary")),
        name="gated_deltanet",
    )(proj, proj, proj, proj, proj, proj, proj, proj, conv_w, alog, dtb, gain)


def _pad_lanes(v, fill=0.0):
    return jnp.concatenate([v, jnp.full((HEAD_DIM - v.shape[0],), fill, F32)])[None, :]


def kernel(x, attn_norm, w_in, q_norm, k_norm, ret_norm, conv_w, a_log, dt_bias, gdn_norm,
           w_branch, w_out, ffn_norm, w_gate, w_up, w_down):
    b, t, d = x.shape
    n = b * t
    depth = w_in.shape[0]
    xf = x.reshape(n, d)

    rope_freq = ROPE_THETA ** (-jnp.arange(0, ROPE_DIM, 2, dtype=F32) / ROPE_DIM)
    ret_freq = RET_THETA ** (-jnp.linspace(0.0, 1.0, HEAD_DIM // 2, dtype=F32))
    cos_a, sin_a = _rotary_tables(t, rope_freq, ROPE_DIM)
    cos_b, sin_b = _rotary_tables(t, ret_freq, HEAD_DIM)
    decay, zeta, xi, gamma = _retention_tables()
    w_proj = w_in_prep(w_in)

    for layer in range(depth):
        proj = in_proj(xf, attn_norm[layer][None, :], w_proj, layer, gates=False)
        gates = in_proj(xf, attn_norm[layer][None, :], w_proj, layer, gates=True)

        k3, vt3, kmean = moba_kprep(proj, k_norm[layer][None, :], cos_a, sin_a, b, t)
        o_a = moba_attention(proj, q_norm[layer][None, :], cos_a, sin_a, k3, vt3,
                             kmean, b, t)

        o_b = retention(proj, cos_b, sin_b, decay, zeta, xi, gamma,
                        ret_norm[layer].reshape(B_HEADS, 1, HEAD_DIM), b, t)

        alog_v = jnp.concatenate([jnp.zeros((C_HEADS,), F32), a_log[layer]])
        dtb_v = jnp.concatenate([jnp.zeros((C_HEADS,), F32), dt_bias[layer]])
        o_c = gated_deltanet(proj, conv_w[layer], _pad_lanes(alog_v), _pad_lanes(dtb_v),
                             gdn_norm[layer][None, :], b, t)

        wbr = w_branch[layer].astype(BF16)
        merged = merge_branches(o_a, o_b, o_c, wbr[:A_WIDTH], wbr[A_WIDTH:A_WIDTH + B_WIDTH],
                                wbr[A_WIDTH + B_WIDTH:], gates)
        xf = residual_matmul(merged, w_out[layer].astype(BF16), xf, 1024, 1024, "out_proj")

        hidden = ffn_up(xf, ffn_norm[layer][None, :], w_gate[layer].astype(BF16), w_up[layer].astype(BF16))
        xf = residual_matmul(hidden, w_down[layer].astype(BF16), xf, 1024, 512, "ffn_down")
    return xf.reshape(b, t, d)
```

```python
import functools

import jax
import jax.numpy as jnp
import numpy as np
from jax import lax
from jax.experimental import pallas as pl
from jax.experimental.pallas import tpu as pltpu

F32 = jnp.float32
BF16 = jnp.bfloat16

D_MODEL = 2048
HEAD_DIM = 128
A_HEADS = 6
B_HEADS = 5
C_HEADS = 5
A_WIDTH = A_HEADS * HEAD_DIM
B_WIDTH = B_HEADS * HEAD_DIM
C_WIDTH = C_HEADS * HEAD_DIM
N_BRANCHES = 3
MOBA_BLOCK = 256
MOBA_TOPK = 3
ROPE_THETA = 500000.0
ROPE_DIM = HEAD_DIM // 4
RET_THETA = 10000.0
RET_CHUNK = 128
GDN_CHUNK = 64
CONV_WIDTH = 4
NORM_EPS = 1e-6

BLK_QA, BLK_KA, BLK_VA = 0, 6, 12
BLK_BA = 18
BLK_QC, BLK_KC, BLK_VC, BLK_ZC = 20, 25, 30, 35
BLK_QB, BLK_KB, BLK_VB, BLK_GB = 40, 45, 50, 55
BLK_GATE = 60
N_BA = 2 * C_HEADS
GATE_COL0 = BLK_GATE * HEAD_DIM
PROJ_BLOCKS = BLK_GATE + N_BRANCHES * D_MODEL // HEAD_DIM
SRC_BLK_B, SRC_BLK_C, SRC_BLK_BA = 18, 38, 58

GDN_GROUP = 4 * GDN_CHUNK
NEG_BIG = -1e30
LOG2_E = 1.4426950408889634
VMEM_LIMIT = 52 * 1024 * 1024


def _cparams(sem):
    return pltpu.CompilerParams(dimension_semantics=sem, vmem_limit_bytes=VMEM_LIMIT)


def _sigmoid(x):
    return 0.5 * jnp.tanh(0.5 * x) + 0.5


def _silu(x):
    return x * _sigmoid(x)


def _dot(a, b, precision=None):
    return jnp.dot(a, b, preferred_element_type=F32, precision=precision)


def _dot_nt(a, b, precision=None):
    return lax.dot_general(a, b, (((1,), (1,)), ((), ())), preferred_element_type=F32, precision=precision)


def _w_in_src_col(j):
    hd = HEAD_DIM
    return jnp.where(
        j < BLK_BA, j * hd,
        jnp.where(j < BLK_QC, SRC_BLK_BA * hd,
                  jnp.where(j < BLK_QB, (j - BLK_QC + SRC_BLK_C) * hd,
                            jnp.where(j < BLK_GATE, (j - BLK_QB + SRC_BLK_B) * hd,
                                      (j - BLK_GATE + SRC_BLK_BA) * hd + N_BA))))


def _w_in_prep_kernel(a_ref, o_ref):
    j = pl.program_id(0)
    depth = o_ref.shape[0]
    special = (j == BLK_BA) | (j == BLK_BA + 1)

    @pl.when(jnp.logical_not(special))
    def _():
        for l in range(depth):
            o_ref[l] = a_ref[:, l, :].T.astype(BF16)

    @pl.when(special)
    def _():
        row = lax.broadcasted_iota(jnp.int32, (a_ref.shape[0], a_ref.shape[2]), 0)
        keep = (row < N_BA) & (j == BLK_BA)
        for l in range(depth):
            o_ref[l] = jnp.where(keep, a_ref[:, l, :], 0.0).T.astype(BF16)


def w_in_prep(w_in):
    depth, d, _ = w_in.shape
    return pl.pallas_call(
        _w_in_prep_kernel,
        grid=(PROJ_BLOCKS,),
        in_specs=[pl.BlockSpec((pl.Element(HEAD_DIM), pl.Element(depth), pl.Element(d)),
                               lambda j: (_w_in_src_col(j), 0, 0))],
        out_specs=pl.BlockSpec((depth, d, HEAD_DIM), lambda j: (0, 0, j)),
        out_shape=jax.ShapeDtypeStruct((depth, d, PROJ_BLOCKS * HEAD_DIM), BF16),
        compiler_params=_cparams(("parallel",)),
        name="w_in_prep",
    )(jnp.transpose(w_in, (2, 0, 1)))


def _norm_rows_to(h_ref, x_ref, g_ref, rows_per_step=256):
    n = x_ref.shape[0] // rows_per_step

    def body(r, c):
        rows = pl.ds(pl.multiple_of(r * rows_per_step, rows_per_step), rows_per_step)
        x = x_ref[rows, :]
        ms = jnp.mean(x * x, axis=-1, keepdims=True)
        h_ref[rows, :] = (x * lax.rsqrt(ms + NORM_EPS) * g_ref[...]).astype(BF16)
        return c

    lax.fori_loop(0, n, body, 0)


def _in_proj_kernel(x_ref, g_ref, w_ref, o_ref, h_ref, *, gates):
    @pl.when(pl.program_id(1) == 0)
    def _():
        _norm_rows_to(h_ref, x_ref, g_ref)

    y = _dot(h_ref[...], w_ref[0])
    o_ref[...] = _sigmoid(y).astype(o_ref.dtype) if gates else y


def in_proj(x, gain, w, layer, gates, tm=1024, tn=1536):
    n, d = x.shape
    col0, cols, dtype = (GATE_COL0, w.shape[2] - GATE_COL0, BF16) if gates else (0, GATE_COL0, F32)
    assert col0 % tn == 0 and cols % tn == 0
    return pl.pallas_call(
        functools.partial(_in_proj_kernel, gates=gates),
        grid=(n // tm, cols // tn),
        in_specs=[
            pl.BlockSpec((tm, d), lambda i, j: (i, 0)),
            pl.BlockSpec((1, d), lambda i, j: (0, 0)),
            pl.BlockSpec((1, d, tn), lambda i, j: (layer, 0, col0 // tn + j)),
        ],
        out_specs=pl.BlockSpec((tm, tn), lambda i, j: (i, j)),
        out_shape=jax.ShapeDtypeStruct((n, cols), dtype),
        scratch_shapes=[pltpu.VMEM((tm, d), BF16)],
        compiler_params=_cparams(("parallel", "arbitrary")),
        name="in_proj_gates" if gates else "in_proj",
    )(x, gain, w)


def _ffn_up_kernel(x_ref, g_ref, wg_ref, wu_ref, o_ref, h_ref):
    @pl.when(pl.program_id(1) == 0)
    def _():
        _norm_rows_to(h_ref, x_ref, g_ref)

    h = h_ref[...]
    o_ref[...] = (_silu(_dot(h, wg_ref[...])) * _dot(h, wu_ref[...])).astype(BF16)


def ffn_up(x, gain, wg, wu, tm=1024, tn=512):
    n, d = x.shape
    cols = wg.shape[1]
    return pl.pallas_call(
        _ffn_up_kernel,
        grid=(n // tm, cols // tn),
        in_specs=[
            pl.BlockSpec((tm, d), lambda i, j: (i, 0)),
            pl.BlockSpec((1, d), lambda i, j: (0, 0)),
            pl.BlockSpec((d, tn), lambda i, j: (0, j)),
            pl.BlockSpec((d, tn), lambda i, j: (0, j)),
        ],
        out_specs=pl.BlockSpec((tm, tn), lambda i, j: (i, j)),
        out_shape=jax.ShapeDtypeStruct((n, cols), BF16),
        scratch_shapes=[pltpu.VMEM((tm, d), BF16)],
        compiler_params=_cparams(("parallel", "arbitrary")),
        name="ffn_up",
    )(x, gain, wg, wu)


def _residual_matmul_kernel(a_ref, w_ref, x_ref, o_ref):
    o_ref[...] = x_ref[...] + _dot(a_ref[...], w_ref[...])


def residual_matmul(a, w, x, tm, tn, name):
    n, k = a.shape
    d = w.shape[1]
    return pl.pallas_call(
        _residual_matmul_kernel,
        grid=(n // tm, d // tn),
        in_specs=[
            pl.BlockSpec((tm, k), lambda i, j: (i, 0)),
            pl.BlockSpec((k, tn), lambda i, j: (0, j)),
            pl.BlockSpec((tm, tn), lambda i, j: (i, j)),
        ],
        out_specs=pl.BlockSpec((tm, tn), lambda i, j: (i, j)),
        out_shape=jax.ShapeDtypeStruct((n, d), F32),
        compiler_params=_cparams(("parallel", "parallel")),
        name=name,
    )(a, w, x)


def _merge_kernel(oa_ref, ob_ref, oc_ref, wa_ref, wb_ref, wc_ref, ga_ref, gb_ref, gc_ref, o_ref):
    merged = ga_ref[...].astype(F32) * _dot(oa_ref[...], wa_ref[...])
    merged += gb_ref[...].astype(F32) * _dot(ob_ref[...], wb_ref[...])
    merged += gc_ref[...].astype(F32) * _dot(oc_ref[...], wc_ref[...])
    o_ref[...] = merged.astype(BF16)


def merge_branches(o_a, o_b, o_c, wa, wb, wc, gates, tm=1024, tn=1024):
    n = o_a.shape[0]
    assert D_MODEL % tn == 0
    per_branch = D_MODEL // tn

    def gate_spec(br):
        return pl.BlockSpec((tm, tn), lambda i, j: (i, br * per_branch + j))

    return pl.pallas_call(
        _merge_kernel,
        grid=(n // tm, D_MODEL // tn),
        in_specs=[
            pl.BlockSpec((tm, A_WIDTH), lambda i, j: (i, 0)),
            pl.BlockSpec((tm, B_WIDTH), lambda i, j: (i, 0)),
            pl.BlockSpec((tm, C_WIDTH), lambda i, j: (i, 0)),
            pl.BlockSpec((A_WIDTH, tn), lambda i, j: (0, j)),
            pl.BlockSpec((B_WIDTH, tn), lambda i, j: (0, j)),
            pl.BlockSpec((C_WIDTH, tn), lambda i, j: (0, j)),
            gate_spec(0), gate_spec(1), gate_spec(2),
        ],
        out_specs=pl.BlockSpec((tm, tn), lambda i, j: (i, j)),
        out_shape=jax.ShapeDtypeStruct((n, D_MODEL), BF16),
        compiler_params=_cparams(("parallel", "parallel")),
        name="merge_branches",
    )(o_a, o_b, o_c, wa, wb, wc, gates, gates, gates)


def _rope_partial(x, cos, sin):
    half = ROPE_DIM // 2
    lane = lax.broadcasted_iota(jnp.int32, x.shape, 1)
    partner = jnp.where(lane < half, pltpu.roll(x, HEAD_DIM - half, 1), pltpu.roll(x, half, 1))
    return x * cos + partner * sin


def _rope_full(x, cos, sin):
    return x * cos + pltpu.roll(x, HEAD_DIM // 2, 1) * sin


def _rotary_tables(t, inv_freq, rot):
    pos = jnp.arange(t, dtype=F32)
    ang = pos[:, None] * inv_freq[None, :]
    ang = jnp.concatenate([ang, ang], axis=-1)
    cos, sin = jnp.cos(ang), jnp.sin(ang)
    sign = jnp.concatenate([-jnp.ones((rot // 2,), F32), jnp.ones((rot // 2,), F32)])
    pad = HEAD_DIM - rot
    cos = jnp.concatenate([cos, jnp.ones((t, pad), F32)], axis=-1)
    sin = jnp.concatenate([sin * sign[None, :], jnp.zeros((t, pad), F32)], axis=-1)
    return cos, sin


def _moba_kprep_kernel(k_ref, v_ref, g_ref, cos_ref, sin_ref, k3_ref, v3_ref, kmean_ref):
    nb = k3_ref.shape[0]
    for j in range(nb):
        rows = slice(j * MOBA_BLOCK, (j + 1) * MOBA_BLOCK)
        x = k_ref[rows, :]
        ms = jnp.mean(x * x, axis=-1, keepdims=True)
        kn = x * lax.rsqrt(ms + NORM_EPS) * g_ref[...]
        kr = _rope_partial(kn, cos_ref[rows, :], sin_ref[rows, :])
        k3_ref[j] = kr.astype(BF16)
        kmean_ref[0, j:j + 1, :] = jnp.mean(kr, axis=0, keepdims=True)
        v3_ref[j] = v_ref[rows, :].astype(BF16)


def moba_kprep(proj, k_gain, cos, sin, b, t):
    nb = t // MOBA_BLOCK
    bh = b * A_HEADS
    return pl.pallas_call(
        _moba_kprep_kernel,
        grid=(b, A_HEADS),
        in_specs=[
            pl.BlockSpec((t, HEAD_DIM), lambda i, h: (i, BLK_KA + h)),
            pl.BlockSpec((t, HEAD_DIM), lambda i, h: (i, BLK_VA + h)),
            pl.BlockSpec((1, HEAD_DIM), lambda i, h: (0, 0)),
            pl.BlockSpec((t, HEAD_DIM), lambda i, h: (0, 0)),
            pl.BlockSpec((t, HEAD_DIM), lambda i, h: (0, 0)),
        ],
        out_specs=[
            pl.BlockSpec((nb, MOBA_BLOCK, HEAD_DIM), lambda i, h: (i * A_HEADS + h, 0, 0)),
            pl.BlockSpec((nb, MOBA_BLOCK, HEAD_DIM), lambda i, h: (i * A_HEADS + h, 0, 0)),
            pl.BlockSpec((1, nb, HEAD_DIM), lambda i, h: (i * A_HEADS + h, 0, 0)),
        ],
        out_shape=[
            jax.ShapeDtypeStruct((bh * nb, MOBA_BLOCK, HEAD_DIM), BF16),
            jax.ShapeDtypeStruct((bh * nb, MOBA_BLOCK, HEAD_DIM), BF16),
            jax.ShapeDtypeStruct((bh, nb, HEAD_DIM), F32),
        ],
        compiler_params=_cparams(("parallel", "parallel")),
        name="moba_kprep",
    )(proj, proj, k_gain, cos, sin)


def _moba_attn_kernel(q_ref, g_ref, cos_ref, sin_ref, k3_ref, v3_ref, kmean_ref, o_ref,
                      qa_ref, m_ref, l_ref, acc_ref):
    nb = kmean_ref.shape[1]
    qi = pl.program_id(1)
    d = HEAD_DIM
    heads = range(A_HEADS)
    blk_rows = MOBA_BLOCK
    ones_cols = jnp.ones((blk_rows, d), BF16)
    lane = lax.broadcasted_iota(jnp.int32, (blk_rows, d), 1)

    def probs(h, s, first):
        m_blk = jnp.max(s, axis=-1, keepdims=True)
        if first:
            m_new, alpha = jnp.broadcast_to(m_blk, (blk_rows, d)), None
        else:
            m_old = m_ref[h]
            m_new = jnp.maximum(m_old, m_blk)
            alpha = jnp.exp2(m_old - m_new)
        m_ref[h] = m_new
        p = jnp.concatenate([jnp.exp2(s[:, :d] - m_new), jnp.exp2(s[:, d:] - m_new)], axis=1)
        return p.astype(BF16), alpha

    def accumulate(h, p, alpha, blk_idx):
        pv = _dot(p, jnp.concatenate([v3_ref[blk_idx], ones_cols], axis=1))
        if alpha is None:
            l_ref[h] = pv[:, d:]
            acc_ref[h] = pv[:, :d]
        else:
            l_ref[h] = alpha * l_ref[h] + pv[:, d:]
            acc_ref[h] = alpha * acc_ref[h] + pv[:, :d]

    def pipelined(items, score, first):
        ahead = 3 if len(items) <= A_HEADS else 2
        s = {i: score(*items[i]) for i in range(min(ahead, len(items)))}
        for i, (h, j, _) in enumerate(items):
            p, alpha = probs(h, s.pop(i), first)
            if i + ahead < len(items):
                s[i + ahead] = score(*items[i + ahead])
            accumulate(h, p, alpha, h * nb + j)

    qr = []
    for h in heads:
        x = q_ref[:, h * d:(h + 1) * d]
        ms = jnp.mean(x * x, axis=-1, keepdims=True)
        qn = x * lax.rsqrt(ms + NORM_EPS) * g_ref[...]
        qr.append(_rope_partial(qn, cos_ref[...], sin_ref[...]))
    gates = [_dot_nt(kmean_ref[h], qr[h], precision=lax.Precision.HIGHEST) for h in heads]
    blk = lax.broadcasted_iota(jnp.int32, gates[0].shape, 0)
    past = blk < qi
    for h in heads:
        gate = jnp.where(past, gates[h], -jnp.inf)
        rank = jnp.zeros(gate.shape, F32)
        for i in range(nb):
            gi = gate[i:i + 1, :]
            wins = jnp.where(gi > gate, 1.0, jnp.where(gi == gate, jnp.where(blk > i, 1.0, 0.0), 0.0))
            rank = rank + wins
        bias = jnp.where(past, jnp.where(rank < MOBA_TOPK, 0.0, NEG_BIG), NEG_BIG)
        bias_q = jnp.concatenate([bias, jnp.zeros((d - nb, blk_rows), F32)], axis=0).T
        qs = qr[h] * (d ** -0.5 * LOG2_E)
        qa_ref[h] = jnp.concatenate([qs.astype(BF16), bias_q.astype(BF16)], axis=1)

    q_pos = lax.broadcasted_iota(jnp.int32, (blk_rows, blk_rows), 0)
    key_pos = lax.broadcasted_iota(jnp.int32, (blk_rows, blk_rows), 1)
    causal = key_pos <= q_pos

    def own_score(h, j, _):
        return jnp.where(causal, _dot_nt(qa_ref[h][:, :d], k3_ref[h * nb + j]), NEG_BIG)

    pipelined([(h, qi, None) for h in heads], own_score, True)

    def past_blocks(js):
        def past_score(h, j, onehot):
            return _dot_nt(qa_ref[h], jnp.concatenate([k3_ref[h * nb + j], onehot], axis=1))

        onehots = [jnp.where(lane == j, 1.0, 0.0).astype(BF16) for j in js]
        pipelined([(h, j, oh) for j, oh in zip(js, onehots) for h in heads], past_score, False)

    def body(jj, carry):
        past_blocks([2 * jj, 2 * jj + 1])
        return carry

    lax.fori_loop(0, qi // 2, body, 0)

    @pl.when(qi % 2 == 1)
    def _():
        past_blocks([qi - 1])
    for h in heads:
        o_ref[:, h * d:(h + 1) * d] = (acc_ref[h] / l_ref[h]).astype(BF16)


def moba_attention(proj, q_gain, cos, sin, k3, v3, kmean, b, t):
    nb = t // MOBA_BLOCK
    n = b * t
    hb = A_HEADS * nb
    stat = pltpu.VMEM((A_HEADS, MOBA_BLOCK, HEAD_DIM), F32)
    return pl.pallas_call(
        _moba_attn_kernel,
        grid=(b, nb),
        in_specs=[
            pl.BlockSpec((MOBA_BLOCK, A_WIDTH), lambda i, q: (i * nb + q, BLK_QA // A_HEADS)),
            pl.BlockSpec((1, HEAD_DIM), lambda i, q: (0, 0)),
            pl.BlockSpec((MOBA_BLOCK, HEAD_DIM), lambda i, q: (q, 0)),
            pl.BlockSpec((MOBA_BLOCK, HEAD_DIM), lambda i, q: (q, 0)),
            pl.BlockSpec((hb, MOBA_BLOCK, HEAD_DIM), lambda i, q: (i, 0, 0)),
            pl.BlockSpec((hb, MOBA_BLOCK, HEAD_DIM), lambda i, q: (i, 0, 0)),
            pl.BlockSpec((A_HEADS, nb, HEAD_DIM), lambda i, q: (i, 0, 0)),
        ],
        out_specs=pl.BlockSpec((MOBA_BLOCK, A_WIDTH), lambda i, q: (i * nb + q, 0)),
        out_shape=jax.ShapeDtypeStruct((n, A_WIDTH), BF16),
        scratch_shapes=[pltpu.VMEM((A_HEADS, MOBA_BLOCK, 2 * HEAD_DIM), BF16), stat, stat, stat],
        compiler_params=_cparams(("parallel", "arbitrary")),
        name="moba_attention",
    )(proj, q_gain, cos, sin, k3, v3, kmean)


def _retention_kernel(q_ref, k_ref, v_ref, gate_ref, cos_ref, sin_ref, decay_ref, zeta_ref, xi_ref,
                      gamma_ref, gain_ref, o_ref, state_ref):
    @pl.when(pl.program_id(1) == 0)
    def _():
        state_ref[...] = jnp.zeros_like(state_ref)

    c = RET_CHUNK
    d = HEAD_DIM
    heads = range(B_HEADS)
    hcols = [slice(h * d, (h + 1) * d) for h in heads]
    state = [state_ref[h] for h in heads]
    for n in range(q_ref.shape[0] // c):
        rows = slice(n * c, (n + 1) * c)
        cos, sin = cos_ref[rows, :], sin_ref[rows, :]
        k = [_rope_full(k_ref[rows, hcols[h]], cos, sin) * (d ** -0.5) for h in heads]
        qb = [_rope_full(q_ref[rows, hcols[h]], cos, sin).astype(BF16) for h in heads]
        kb = [k[h].astype(BF16) for h in heads]
        vb = [v_ref[rows, hcols[h]].astype(BF16) for h in heads]
        scores = [_dot_nt(qb[h], kb[h]) for h in heads]
        cross = [_dot(qb[h], state[h].astype(BF16)) for h in heads]
        kz_t = [(k[h] * zeta_ref[h]).T.astype(BF16) for h in heads]
        inner = [_dot((scores[h] * decay_ref[h]).astype(BF16), vb[h]) for h in heads]
        update = [_dot(kz_t[h], vb[h]) for h in heads]
        for h in heads:
            state[h] = state[h] * gamma_ref[h, 0:1, :] + update[h]
            ret = inner[h] + cross[h] * xi_ref[h]
            ms = jnp.mean(ret * ret, axis=-1, keepdims=True)
            y = ret * lax.rsqrt(ms + NORM_EPS) * gain_ref[h]
            o_ref[rows, hcols[h]] = (y * _silu(gate_ref[rows, hcols[h]])).astype(BF16)
    for h in heads:
        state_ref[h] = state[h]


def retention(proj, cos, sin, decay, zeta, xi, gamma, gain, b, t, rows=512):
    n = b * t
    steps = t // rows
    b_blk = BLK_QB // B_HEADS

    def cur(off):
        return pl.BlockSpec((rows, B_WIDTH), lambda i, s: (i * steps + s, b_blk + off))

    def table(shape):
        return pl.BlockSpec((B_HEADS,) + shape, lambda i, s: (0, 0, 0))

    return pl.pallas_call(
        _retention_kernel,
        grid=(b, steps),
        in_specs=[
            cur(0), cur(1), cur(2), cur(3),
            pl.BlockSpec((rows, HEAD_DIM), lambda i, s: (s, 0)),
            pl.BlockSpec((rows, HEAD_DIM), lambda i, s: (s, 0)),
            table((RET_CHUNK, RET_CHUNK)), table((RET_CHUNK, HEAD_DIM)), table((RET_CHUNK, HEAD_DIM)),
            table((8, HEAD_DIM)), table((1, HEAD_DIM)),
        ],
        out_specs=pl.BlockSpec((rows, B_WIDTH), lambda i, s: (i * steps + s, 0)),
        out_shape=jax.ShapeDtypeStruct((n, B_WIDTH), BF16),
        scratch_shapes=[pltpu.VMEM((B_HEADS, HEAD_DIM, HEAD_DIM), F32)],
        compiler_params=_cparams(("parallel", "arbitrary")),
        name="retention",
    )(proj, proj, proj, proj, cos, sin, decay, zeta, xi, gamma, gain)


def _retention_tables():
    c = RET_CHUNK
    log_gamma = jnp.log1p(-jnp.exp2(-5.0 - jnp.arange(B_HEADS, dtype=F32)))
    i = jnp.arange(c, dtype=F32)
    lg = log_gamma[:, None]
    tril = jnp.tril(jnp.ones((c, c), dtype=bool))
    rel = i[:, None] - i[None, :]
    decay = jnp.where(tril, jnp.exp(jnp.where(tril, lg[:, :, None] * rel, 0.0)), 0.0)
    zeta = jnp.exp(lg * (c - 1 - i))
    xi = jnp.exp(lg * (i + 1.0))
    gamma_c = jnp.exp(log_gamma * c)
    bcast = lambda v: jnp.broadcast_to(v[:, :, None], (B_HEADS, c, HEAD_DIM))
    gamma = jnp.broadcast_to(gamma_c[:, None, None], (B_HEADS, 8, HEAD_DIM))
    return decay, bcast(zeta), bcast(xi), gamma


def _lane_column(x, idx):
    lane = lax.broadcasted_iota(jnp.int32, x.shape, 1)
    col = jnp.sum(jnp.where(lane == idx, x, 0.0), axis=-1, keepdims=True)
    return jnp.broadcast_to(col, x.shape)


def _gdn_kernel(q_ref, k_ref, v_ref, qp_ref, kp_ref, vp_ref, z_ref, ba_ref, cw_ref, alog_ref, dtb_ref,
                gain_ref, o_ref, state_ref):
    r = GDN_GROUP
    c = GDN_CHUNK
    d = HEAD_DIM
    w = C_WIDTH
    hi = lax.Precision.HIGHEST
    first = pl.program_id(1) == 0

    @pl.when(first)
    def _():
        state_ref[...] = jnp.zeros_like(state_ref)

    row = lax.broadcasted_iota(jnp.int32, (r, r), 0)
    col = lax.broadcasted_iota(jnp.int32, (r, r), 1)
    same_chunk = (row // c) == (col // c)
    tril = same_chunk & (row >= col)
    strict = same_chunk & (row > col)
    lane_r = lax.broadcasted_iota(jnp.int32, (d, r), 1)

    def conv_silu(x_ref, prev_ref, w0):
        prev = jnp.where(first, 0.0, prev_ref[...])
        xe = jnp.concatenate([prev, x_ref[...]], axis=0)
        y = xe[8:, :] * cw_ref[CONV_WIDTH - 1:CONV_WIDTH, w0:w0 + w]
        for back in range(1, CONV_WIDTH):
            tap = CONV_WIDTH - 1 - back
            y = y + pltpu.roll(xe, back, 0)[8:, :] * cw_ref[tap:tap + 1, w0:w0 + w]
        return _silu(y)

    xq = conv_silu(q_ref, qp_ref, 0)
    xk = conv_silu(k_ref, kp_ref, w)
    xv = conv_silu(v_ref, vp_ref, 2 * w)

    ba = ba_ref[...]
    beta_all = _sigmoid(ba)
    pre = ba + dtb_ref[...]
    softplus = jnp.maximum(pre, 0.0) + jnp.log(1.0 + jnp.exp(-jnp.abs(pre)))
    g_all = -jnp.exp(alog_ref[...]) * softplus
    gcum_all = _dot(jnp.where(tril, 1.0, 0.0), g_all, precision=hi)
    gtot_all = _dot(jnp.where(same_chunk, 1.0, 0.0), g_all, precision=hi)

    heads = range(C_HEADS)
    hcols = [slice(h * d, (h + 1) * d) for h in heads]
    q, k, k_beta, kb16, lmask, exp_g, gcum, gtot, x = ([None] * C_HEADS for _ in range(9))
    for h in heads:
        xqh, xkh, v = xq[:, hcols[h]], xk[:, hcols[h]], xv[:, hcols[h]]
        q[h] = xqh * lax.rsqrt(jnp.sum(xqh * xqh, axis=-1, keepdims=True) + NORM_EPS) * (d ** -0.5)
        k[h] = xkh * lax.rsqrt(jnp.sum(xkh * xkh, axis=-1, keepdims=True) + NORM_EPS)
        beta = _lane_column(beta_all, h)
        gcum[h] = _lane_column(gcum_all, C_HEADS + h)
        gtot[h] = _lane_column(gtot_all, C_HEADS + h)
        gcum_col = jnp.concatenate([gcum[h], gcum[h]], axis=1)
        diff = gcum_col - gcum_col.T
        lmask[h] = jnp.where(tril, jnp.exp(jnp.where(tril, diff, 0.0)), 0.0)
        exp_g[h] = jnp.exp(gcum[h])
        k_beta[h] = k[h] * beta
        kb16[h] = k[h].astype(BF16)
        x[h] = jnp.concatenate([v * beta, k_beta[h] * exp_g[h]], axis=1)

    kk = [_dot_nt(k_beta[h].astype(BF16), kb16[h]) for h in heads]
    qk = [_dot_nt(q[h].astype(BF16), kb16[h]) for h in heads]
    p = [-jnp.where(strict, kk[h] * lmask[h], 0.0) for h in heads]
    attn = [(qk[h] * lmask[h]).astype(BF16) for h in heads]

    steps = int(np.log2(c))
    for s in range(steps):
        p16 = [p[h].astype(BF16) for h in heads]
        x = [x[h] + _dot(p16[h], x[h].astype(BF16)) for h in heads]
        if s + 1 < steps:
            p = [_dot(p16[h], p16[h]) for h in heads]

    u = [x[h][:, :d] for h in heads]
    w16 = [x[h][:, d:].astype(BF16) for h in heads]
    qdec16 = [(q[h] * exp_g[h]).astype(BF16) for h in heads]
    kdec_t = [(k[h] * jnp.exp(gtot[h] - gcum[h])).T for h in heads]
    chunk_decay = [jnp.exp(gtot[h]) for h in heads]

    state = [state_ref[h] for h in heads]
    vnew = [[] for _ in heads]
    for n in range(r // c):
        rows = slice(n * c, (n + 1) * c)
        s16 = [state[h].astype(BF16) for h in heads]
        ws = [_dot(w16[h][rows, :], s16[h]) for h in heads]
        qs = [_dot(qdec16[h][rows, :], s16[h]) for h in heads]
        vfull = []
        for h in heads:
            vnew[h].append((u[h][rows, :] - ws[h]).astype(BF16))
            vfull.append(jnp.concatenate(vnew[h] + [jnp.zeros((c, d), BF16)] * (r // c - 1 - n), axis=0))
        av = [_dot(attn[h][rows, :], vfull[h]) for h in heads]
        kd = [jnp.where((lane_r // c) == n, kdec_t[h], 0.0).astype(BF16) for h in heads]
        kv = [_dot(kd[h], vfull[h]) for h in heads]
        for h in heads:
            state[h] = state[h] * chunk_decay[h][n * c:n * c + 1, :] + kv[h]
            o = qs[h] + av[h]
            ms = jnp.mean(o * o, axis=-1, keepdims=True)
            y = o * lax.rsqrt(ms + NORM_EPS) * gain_ref[...]
            o_ref[rows, hcols[h]] = (y * _silu(z_ref[rows, hcols[h]])).astype(BF16)
    for h in heads:
        state_ref[h] = state[h]


def gated_deltanet(proj, conv_w, alog, dtb, gain, b, t):
    n = b * t
    r = GDN_GROUP
    groups = t // r
    c_blk = BLK_QC // C_HEADS

    def cur(off):
        return pl.BlockSpec((r, C_WIDTH), lambda i, g: (i * groups + g, c_blk + off))

    def prev(off):
        return pl.BlockSpec((8, C_WIDTH), lambda i, g: (jnp.maximum((i * groups + g) * (r // 8) - 1, 0), c_blk + off))

    vec = pl.BlockSpec((1, HEAD_DIM), lambda i, g: (0, 0))
    return pl.pallas_call(
        _gdn_kernel,
        grid=(b, groups),
        in_specs=[
            cur(0), cur(1), cur(2), prev(0), prev(1), prev(2), cur(3),
            pl.BlockSpec((r, HEAD_DIM), lambda i, g: (i * groups + g, BLK_BA)),
            pl.BlockSpec((CONV_WIDTH, 3 * C_WIDTH), lambda i, g: (0, 0)),
            vec, vec, vec,
        ],
        out_specs=pl.BlockSpec((r, C_WIDTH), lambda i, g: (i * groups + g, 0)),
        out_shape=jax.ShapeDtypeStruct((n, C_WIDTH), BF16),
        scratch_shapes=[pltpu.VMEM((C_HEADS, HEAD_DIM, HEAD_DIM), F32)],
        compiler_params=_cparams(("parallel", "arbitrary")),
        name="gated_deltanet",
    )(proj, proj, proj, proj, proj, proj, proj, proj, conv_w, alog, dtb, gain)


def _pad_lanes(v, fill=0.0):
    return jnp.concatenate([v, jnp.full((HEAD_DIM - v.shape[0],), fill, F32)])[None, :]


def kernel(x, attn_norm, w_in, q_norm, k_norm, ret_norm, conv_w, a_log, dt_bias, gdn_norm,
           w_branch, w_out, ffn_norm, w_gate, w_up, w_down):
    b, t, d = x.shape
    n = b * t
    depth = w_in.shape[0]
    xf = x.reshape(n, d)

    rope_freq = ROPE_THETA ** (-jnp.arange(0, ROPE_DIM, 2, dtype=F32) / ROPE_DIM)
    ret_freq = RET_THETA ** (-jnp.linspace(0.0, 1.0, HEAD_DIM // 2, dtype=F32))
    cos_a, sin_a = _rotary_tables(t, rope_freq, ROPE_DIM)
    cos_b, sin_b = _rotary_tables(t, ret_freq, HEAD_DIM)
    decay, zeta, xi, gamma = _retention_tables()
    w_proj = w_in_prep(w_in)

    for layer in range(depth):
        proj = in_proj(xf, attn_norm[layer][None, :], w_proj, layer, gates=False)
        gates = in_proj(xf, attn_norm[layer][None, :], w_proj, layer, gates=True)

        k3, vt3, kmean = moba_kprep(proj, k_norm[layer][None, :], cos_a, sin_a, b, t)
        o_a = moba_attention(proj, q_norm[layer][None, :], cos_a, sin_a, k3, vt3,
                             kmean, b, t)

        o_b = retention(proj, cos_b, sin_b, decay, zeta, xi, gamma,
                        ret_norm[layer].reshape(B_HEADS, 1, HEAD_DIM), b, t)

        alog_v = jnp.concatenate([jnp.zeros((C_HEADS,), F32), a_log[layer]])
        dtb_v = jnp.concatenate([jnp.zeros((C_HEADS,), F32), dt_bias[layer]])
        o_c = gated_deltanet(proj, conv_w[layer], _pad_lanes(alog_v), _pad_lanes(dtb_v),
                             gdn_norm[layer][None, :], b, t)

        wbr = w_branch[layer].astype(BF16)
        merged = merge_branches(o_a, o_b, o_c, wbr[:A_WIDTH], wbr[A_WIDTH:A_WIDTH + B_WIDTH],
                                wbr[A_WIDTH + B_WIDTH:], gates)
        xf = residual_matmul(merged, w_out[layer].astype(BF16), xf, 1024, 1024, "out_proj")

        hidden = ffn_up(xf, ffn_norm[layer][None, :], w_gate[layer].astype(BF16), w_up[layer].astype(BF16))
        xf = residual_matmul(hidden, w_down[layer].astype(BF16), xf, 1024, 512, "ffn_down")
    return xf.reshape(b, t, d)
```

```python
import jax
import jax.numpy as jnp
import numpy as np
from jax import lax
from jax.experimental import pallas as pl
from jax.experimental.pallas import tpu as pltpu

F32 = jnp.float32
BF16 = jnp.bfloat16

D_MODEL = 2048
HEAD_DIM = 128
A_HEADS = 6
B_HEADS = 5
C_HEADS = 5
A_WIDTH = A_HEADS * HEAD_DIM
B_WIDTH = B_HEADS * HEAD_DIM
C_WIDTH = C_HEADS * HEAD_DIM
N_BRANCHES = 3
MOBA_BLOCK = 256
MOBA_TOPK = 3
ROPE_THETA = 500000.0
ROPE_DIM = HEAD_DIM // 4
RET_THETA = 10000.0
RET_CHUNK = 128
GDN_CHUNK = 64
CONV_WIDTH = 4
NORM_EPS = 1e-6

BLK_QA, BLK_KA, BLK_VA = 0, 6, 12
BLK_BA = 18
BLK_QC, BLK_KC, BLK_VC, BLK_ZC = 20, 25, 30, 35
BLK_QB, BLK_KB, BLK_VB, BLK_GB = 40, 45, 50, 55
BLK_GATE = 60
N_BA = 2 * C_HEADS
GATE_COL0 = BLK_GATE * HEAD_DIM
PROJ_BLOCKS = BLK_GATE + N_BRANCHES * D_MODEL // HEAD_DIM
SRC_BLK_B, SRC_BLK_C, SRC_BLK_BA = 18, 38, 58

GDN_GROUP = 4 * GDN_CHUNK
NEG_BIG = -1e30
LOG2_E = 1.4426950408889634
VMEM_LIMIT = 52 * 1024 * 1024


def _cparams(sem):
    return pltpu.CompilerParams(dimension_semantics=sem, vmem_limit_bytes=VMEM_LIMIT)


def _sigmoid(x):
    return 0.5 * jnp.tanh(0.5 * x) + 0.5


def _silu(x):
    return x * _sigmoid(x)


def _dot(a, b, precision=None):
    return jnp.dot(a, b, preferred_element_type=F32, precision=precision)


def _dot_nt(a, b, precision=None):
    return lax.dot_general(a, b, (((1,), (1,)), ((), ())), preferred_element_type=F32, precision=precision)


def _w_in_src_col(j):
    hd = HEAD_DIM
    return jnp.where(
        j < BLK_BA, j * hd,
        jnp.where(j < BLK_QC, SRC_BLK_BA * hd,
                  jnp.where(j < BLK_QB, (j - BLK_QC + SRC_BLK_C) * hd,
                            jnp.where(j < BLK_GATE, (j - BLK_QB + SRC_BLK_B) * hd,
                                      (j - BLK_GATE + SRC_BLK_BA) * hd + N_BA))))


def _w_in_prep_kernel(a_ref, o_ref):
    j = pl.program_id(0)
    depth = o_ref.shape[0]
    special = (j == BLK_BA) | (j == BLK_BA + 1)

    @pl.when(jnp.logical_not(special))
    def _():
        for l in range(depth):
            o_ref[l] = a_ref[:, l, :].T.astype(BF16)

    @pl.when(special)
    def _():
        row = lax.broadcasted_iota(jnp.int32, (a_ref.shape[0], a_ref.shape[2]), 0)
        keep = (row < N_BA) & (j == BLK_BA)
        for l in range(depth):
            o_ref[l] = jnp.where(keep, a_ref[:, l, :], 0.0).T.astype(BF16)


def w_in_prep(w_in):
    depth, d, _ = w_in.shape
    return pl.pallas_call(
        _w_in_prep_kernel,
        grid=(PROJ_BLOCKS,),
        in_specs=[pl.BlockSpec((pl.Element(HEAD_DIM), pl.Element(depth), pl.Element(d)),
                               lambda j: (_w_in_src_col(j), 0, 0))],
        out_specs=pl.BlockSpec((depth, d, HEAD_DIM), lambda j: (0, 0, j)),
        out_shape=jax.ShapeDtypeStruct((depth, d, PROJ_BLOCKS * HEAD_DIM), BF16),
        compiler_params=_cparams(("parallel",)),
        name="w_in_prep",
    )(jnp.transpose(w_in, (2, 0, 1)))


def _norm_rows_to(h_ref, x_ref, g_ref, rows_per_step=256):
    n = x_ref.shape[0] // rows_per_step

    def body(r, c):
        rows = pl.ds(pl.multiple_of(r * rows_per_step, rows_per_step), rows_per_step)
        x = x_ref[rows, :]
        ms = jnp.mean(x * x, axis=-1, keepdims=True)
        h_ref[rows, :] = (x * lax.rsqrt(ms + NORM_EPS) * g_ref[...]).astype(BF16)
        return c

    lax.fori_loop(0, n, body, 0)


def _in_proj_gates_kernel(x_ref, g_ref, w_ref, o_ref, h_ref):
    @pl.when(pl.program_id(1) == 0)
    def _():
        _norm_rows_to(h_ref, x_ref, g_ref)

    o_ref[...] = _sigmoid(_dot(h_ref[...], w_ref[0])).astype(BF16)


def _in_proj_main_kernel(h_ref, w_ref, o_ref):
    o_ref[...] = _dot(h_ref[...], w_ref[0])


def in_proj_gates(x, gain, w, layer, tm=1024, tn=1536):
    n, d = x.shape
    cols = w.shape[2] - GATE_COL0
    assert GATE_COL0 % tn == 0 and cols % tn == 0
    return pl.pallas_call(
        _in_proj_gates_kernel,
        grid=(n // tm, cols // tn),
        in_specs=[
            pl.BlockSpec((tm, d), lambda i, j: (i, 0)),
            pl.BlockSpec((1, d), lambda i, j: (0, 0)),
            pl.BlockSpec((1, d, tn), lambda i, j: (layer, 0, GATE_COL0 // tn + j)),
        ],
        out_specs=[pl.BlockSpec((tm, tn), lambda i, j: (i, j)), pl.BlockSpec((tm, d), lambda i, j: (i, 0))],
        out_shape=[jax.ShapeDtypeStruct((n, cols), BF16), jax.ShapeDtypeStruct((n, d), BF16)],
        compiler_params=_cparams(("parallel", "arbitrary")),
        name="in_proj_gates",
    )(x, gain, w)


def in_proj_main(h, w, layer, tm=1024, tn=1536):
    n, d = h.shape
    assert GATE_COL0 % tn == 0
    return pl.pallas_call(
        _in_proj_main_kernel,
        grid=(n // tm, GATE_COL0 // tn),
        in_specs=[
            pl.BlockSpec((tm, d), lambda i, j: (i, 0)),
            pl.BlockSpec((1, d, tn), lambda i, j: (layer, 0, j)),
        ],
        out_specs=pl.BlockSpec((tm, tn), lambda i, j: (i, j)),
        out_shape=jax.ShapeDtypeStruct((n, GATE_COL0), F32),
        compiler_params=_cparams(("parallel", "parallel")),
        name="in_proj",
    )(h, w)


def _ffn_up_kernel(x_ref, g_ref, wg_ref, wu_ref, o_ref, h_ref):
    @pl.when(pl.program_id(1) == 0)
    def _():
        _norm_rows_to(h_ref, x_ref, g_ref)

    h = h_ref[...]
    o_ref[...] = (_silu(_dot(h, wg_ref[...])) * _dot(h, wu_ref[...])).astype(BF16)


def ffn_up(x, gain, wg, wu, tm=1024, tn=512):
    n, d = x.shape
    cols = wg.shape[1]
    return pl.pallas_call(
        _ffn_up_kernel,
        grid=(n // tm, cols // tn),
        in_specs=[
            pl.BlockSpec((tm, d), lambda i, j: (i, 0)),
            pl.BlockSpec((1, d), lambda i, j: (0, 0)),
            pl.BlockSpec((d, tn), lambda i, j: (0, j)),
            pl.BlockSpec((d, tn), lambda i, j: (0, j)),
        ],
        out_specs=pl.BlockSpec((tm, tn), lambda i, j: (i, j)),
        out_shape=jax.ShapeDtypeStruct((n, cols), BF16),
        scratch_shapes=[pltpu.VMEM((tm, d), BF16)],
        compiler_params=_cparams(("parallel", "arbitrary")),
        name="ffn_up",
    )(x, gain, wg, wu)


def _residual_matmul_kernel(a_ref, w_ref, x_ref, o_ref):
    o_ref[...] = x_ref[...] + _dot(a_ref[...], w_ref[...])


def residual_matmul(a, w, x, tm, tn, name):
    n, k = a.shape
    d = w.shape[1]
    return pl.pallas_call(
        _residual_matmul_kernel,
        grid=(n // tm, d // tn),
        in_specs=[
            pl.BlockSpec((tm, k), lambda i, j: (i, 0)),
            pl.BlockSpec((k, tn), lambda i, j: (0, j)),
            pl.BlockSpec((tm, tn), lambda i, j: (i, j)),
        ],
        out_specs=pl.BlockSpec((tm, tn), lambda i, j: (i, j)),
        out_shape=jax.ShapeDtypeStruct((n, d), F32),
        compiler_params=_cparams(("parallel", "parallel")),
        name=name,
    )(a, w, x)


def _merge_kernel(oa_ref, ob_ref, oc_ref, wa_ref, wb_ref, wc_ref, ga_ref, gb_ref, gc_ref, o_ref):
    merged = ga_ref[...].astype(F32) * _dot(oa_ref[...], wa_ref[...])
    merged += gb_ref[...].astype(F32) * _dot(ob_ref[...], wb_ref[...])
    merged += gc_ref[...].astype(F32) * _dot(oc_ref[...], wc_ref[...])
    o_ref[...] = merged.astype(BF16)


def merge_branches(o_a, o_b, o_c, wa, wb, wc, gates, tm=1024, tn=1024):
    n = o_a.shape[0]
    assert D_MODEL % tn == 0
    per_branch = D_MODEL // tn

    def gate_spec(br):
        return pl.BlockSpec((tm, tn), lambda i, j: (i, br * per_branch + j))

    return pl.pallas_call(
        _merge_kernel,
        grid=(n // tm, D_MODEL // tn),
        in_specs=[
            pl.BlockSpec((tm, A_WIDTH), lambda i, j: (i, 0)),
            pl.BlockSpec((tm, B_WIDTH), lambda i, j: (i, 0)),
            pl.BlockSpec((tm, C_WIDTH), lambda i, j: (i, 0)),
            pl.BlockSpec((A_WIDTH, tn), lambda i, j: (0, j)),
            pl.BlockSpec((B_WIDTH, tn), lambda i, j: (0, j)),
            pl.BlockSpec((C_WIDTH, tn), lambda i, j: (0, j)),
            gate_spec(0), gate_spec(1), gate_spec(2),
        ],
        out_specs=pl.BlockSpec((tm, tn), lambda i, j: (i, j)),
        out_shape=jax.ShapeDtypeStruct((n, D_MODEL), BF16),
        compiler_params=_cparams(("parallel", "parallel")),
        name="merge_branches",
    )(o_a, o_b, o_c, wa, wb, wc, gates, gates, gates)


def _rope_partial(x, cos, sin):
    half = ROPE_DIM // 2
    lane = lax.broadcasted_iota(jnp.int32, x.shape, 1)
    partner = jnp.where(lane < half, pltpu.roll(x, HEAD_DIM - half, 1), pltpu.roll(x, half, 1))
    return x * cos + partner * sin


def _rope_full(x, cos, sin):
    return x * cos + pltpu.roll(x, HEAD_DIM // 2, 1) * sin


def _rotary_tables(t, inv_freq, rot):
    pos = jnp.arange(t, dtype=F32)
    ang = pos[:, None] * inv_freq[None, :]
    ang = jnp.concatenate([ang, ang], axis=-1)
    cos, sin = jnp.cos(ang), jnp.sin(ang)
    sign = jnp.concatenate([-jnp.ones((rot // 2,), F32), jnp.ones((rot // 2,), F32)])
    pad = HEAD_DIM - rot
    cos = jnp.concatenate([cos, jnp.ones((t, pad), F32)], axis=-1)
    sin = jnp.concatenate([sin * sign[None, :], jnp.zeros((t, pad), F32)], axis=-1)
    return cos, sin


def _moba_kprep_kernel(k_ref, v_ref, g_ref, cos_ref, sin_ref, k3_ref, v3_ref, kmean_ref):
    nb = k3_ref.shape[0]
    for j in range(nb):
        rows = slice(j * MOBA_BLOCK, (j + 1) * MOBA_BLOCK)
        x = k_ref[rows, :]
        ms = jnp.mean(x * x, axis=-1, keepdims=True)
        kn = x * lax.rsqrt(ms + NORM_EPS) * g_ref[...]
        kr = _rope_partial(kn, cos_ref[rows, :], sin_ref[rows, :])
        k3_ref[j] = kr.astype(BF16)
        kmean_ref[0, j:j + 1, :] = jnp.mean(kr, axis=0, keepdims=True)
        v3_ref[j] = v_ref[rows, :].astype(BF16)


def moba_kprep(proj, k_gain, cos, sin, b, t):
    nb = t // MOBA_BLOCK
    bh = b * A_HEADS
    return pl.pallas_call(
        _moba_kprep_kernel,
        grid=(b, A_HEADS),
        in_specs=[
            pl.BlockSpec((t, HEAD_DIM), lambda i, h: (i, BLK_KA + h)),
            pl.BlockSpec((t, HEAD_DIM), lambda i, h: (i, BLK_VA + h)),
            pl.BlockSpec((1, HEAD_DIM), lambda i, h: (0, 0)),
            pl.BlockSpec((t, HEAD_DIM), lambda i, h: (0, 0)),
            pl.BlockSpec((t, HEAD_DIM), lambda i, h: (0, 0)),
        ],
        out_specs=[
            pl.BlockSpec((nb, MOBA_BLOCK, HEAD_DIM), lambda i, h: (i * A_HEADS + h, 0, 0)),
            pl.BlockSpec((nb, MOBA_BLOCK, HEAD_DIM), lambda i, h: (i * A_HEADS + h, 0, 0)),
            pl.BlockSpec((1, nb, HEAD_DIM), lambda i, h: (i * A_HEADS + h, 0, 0)),
        ],
        out_shape=[
            jax.ShapeDtypeStruct((bh * nb, MOBA_BLOCK, HEAD_DIM), BF16),
            jax.ShapeDtypeStruct((bh * nb, MOBA_BLOCK, HEAD_DIM), BF16),
            jax.ShapeDtypeStruct((bh, nb, HEAD_DIM), F32),
        ],
        compiler_params=_cparams(("parallel", "parallel")),
        name="moba_kprep",
    )(proj, proj, k_gain, cos, sin)


def _moba_attn_kernel(q_ref, g_ref, cos_ref, sin_ref, k3_ref, v3_ref, kmean_ref, o_ref,
                      qa_ref, m_ref, l_ref, acc_ref):
    nb = kmean_ref.shape[1]
    qi = pl.program_id(1)
    d = HEAD_DIM
    heads = range(A_HEADS)
    blk_rows = MOBA_BLOCK
    ones_cols = jnp.ones((blk_rows, d), BF16)
    lane = lax.broadcasted_iota(jnp.int32, (blk_rows, d), 1)

    def probs(h, s, first):
        m_blk = jnp.max(s, axis=-1, keepdims=True)
        if first:
            m_new, alpha = jnp.broadcast_to(m_blk, (blk_rows, d)), None
        else:
            m_old = m_ref[h]
            m_new = jnp.maximum(m_old, m_blk)
            alpha = jnp.exp2(m_old - m_new)
        m_ref[h] = m_new
        p = jnp.concatenate([jnp.exp2(s[:, :d] - m_new), jnp.exp2(s[:, d:] - m_new)], axis=1)
        return p.astype(BF16), alpha

    def accumulate(h, p, alpha, blk_idx):
        pv = _dot(p, jnp.concatenate([v3_ref[blk_idx], ones_cols], axis=1))
        if alpha is None:
            l_ref[h] = pv[:, d:]
            acc_ref[h] = pv[:, :d]
        else:
            l_ref[h] = alpha * l_ref[h] + pv[:, d:]
            acc_ref[h] = alpha * acc_ref[h] + pv[:, :d]

    def pipelined(items, score, first):
        ahead = 3 if len(items) <= A_HEADS else 2
        s = {i: score(*items[i]) for i in range(min(ahead, len(items)))}
        for i, (h, j, _) in enumerate(items):
            p, alpha = probs(h, s.pop(i), first)
            if i + ahead < len(items):
                s[i + ahead] = score(*items[i + ahead])
            accumulate(h, p, alpha, h * nb + j)

    qr = []
    for h in heads:
        x = q_ref[:, h * d:(h + 1) * d]
        ms = jnp.mean(x * x, axis=-1, keepdims=True)
        qn = x * lax.rsqrt(ms + NORM_EPS) * g_ref[...]
        qr.append(_rope_partial(qn, cos_ref[...], sin_ref[...]))
    gates = [_dot_nt(kmean_ref[h], qr[h], precision=lax.Precision.HIGHEST) for h in heads]
    blk = lax.broadcasted_iota(jnp.int32, gates[0].shape, 0)
    past = blk < qi
    for h in heads:
        gate = jnp.where(past, gates[h], -jnp.inf)
        rank = jnp.zeros(gate.shape, F32)
        for i in range(nb):
            gi = gate[i:i + 1, :]
            wins = jnp.where(gi > gate, 1.0, jnp.where(gi == gate, jnp.where(blk > i, 1.0, 0.0), 0.0))
            rank = rank + wins
        bias = jnp.where(past, jnp.where(rank < MOBA_TOPK, 0.0, NEG_BIG), NEG_BIG)
        bias_q = jnp.concatenate([bias, jnp.zeros((d - nb, blk_rows), F32)], axis=0).T
        qs = qr[h] * (d ** -0.5 * LOG2_E)
        qa_ref[h] = jnp.concatenate([qs.astype(BF16), bias_q.astype(BF16)], axis=1)

    q_pos = lax.broadcasted_iota(jnp.int32, (blk_rows, blk_rows), 0)
    key_pos = lax.broadcasted_iota(jnp.int32, (blk_rows, blk_rows), 1)
    causal = key_pos <= q_pos

    def own_score(h, j, _):
        return jnp.where(causal, _dot_nt(qa_ref[h][:, :d], k3_ref[h * nb + j]), NEG_BIG)

    pipelined([(h, qi, None) for h in heads], own_score, True)

    def past_blocks(js):
        def past_score(h, j, onehot):
            return _dot_nt(qa_ref[h], jnp.concatenate([k3_ref[h * nb + j], onehot], axis=1))

        onehots = [jnp.where(lane == j, 1.0, 0.0).astype(BF16) for j in js]
        pipelined([(h, j, oh) for j, oh in zip(js, onehots) for h in heads], past_score, False)

    def body(jj, carry):
        past_blocks([2 * jj, 2 * jj + 1])
        return carry

    lax.fori_loop(0, qi // 2, body, 0)

    @pl.when(qi % 2 == 1)
    def _():
        past_blocks([qi - 1])
    for h in heads:
        o_ref[:, h * d:(h + 1) * d] = (acc_ref[h] / l_ref[h]).astype(BF16)


def moba_attention(proj, q_gain, cos, sin, k3, v3, kmean, b, t):
    nb = t // MOBA_BLOCK
    n = b * t
    hb = A_HEADS * nb
    stat = pltpu.VMEM((A_HEADS, MOBA_BLOCK, HEAD_DIM), F32)
    return pl.pallas_call(
        _moba_attn_kernel,
        grid=(b, nb),
        in_specs=[
            pl.BlockSpec((MOBA_BLOCK, A_WIDTH), lambda i, q: (i * nb + q, BLK_QA // A_HEADS)),
            pl.BlockSpec((1, HEAD_DIM), lambda i, q: (0, 0)),
            pl.BlockSpec((MOBA_BLOCK, HEAD_DIM), lambda i, q: (q, 0)),
            pl.BlockSpec((MOBA_BLOCK, HEAD_DIM), lambda i, q: (q, 0)),
            pl.BlockSpec((hb, MOBA_BLOCK, HEAD_DIM), lambda i, q: (i, 0, 0)),
            pl.BlockSpec((hb, MOBA_BLOCK, HEAD_DIM), lambda i, q: (i, 0, 0)),
            pl.BlockSpec((A_HEADS, nb, HEAD_DIM), lambda i, q: (i, 0, 0)),
        ],
        out_specs=pl.BlockSpec((MOBA_BLOCK, A_WIDTH), lambda i, q: (i * nb + q, 0)),
        out_shape=jax.ShapeDtypeStruct((n, A_WIDTH), BF16),
        scratch_shapes=[pltpu.VMEM((A_HEADS, MOBA_BLOCK, 2 * HEAD_DIM), BF16), stat, stat, stat],
        compiler_params=_cparams(("parallel", "arbitrary")),
        name="moba_attention",
    )(proj, q_gain, cos, sin, k3, v3, kmean)


def _retention_kernel(q_ref, k_ref, v_ref, gate_ref, cos_ref, sin_ref, decay_ref, zeta_ref, xi_ref,
                      gamma_ref, gain_ref, o_ref, state_ref):
    @pl.when(pl.program_id(1) == 0)
    def _():
        state_ref[...] = jnp.zeros_like(state_ref)

    c = RET_CHUNK
    d = HEAD_DIM
    heads = range(B_HEADS)
    hcols = [slice(h * d, (h + 1) * d) for h in heads]
    state = [state_ref[h] for h in heads]
    for n in range(q_ref.shape[0] // c):
        rows = slice(n * c, (n + 1) * c)
        cos, sin = cos_ref[rows, :], sin_ref[rows, :]
        k = [_rope_full(k_ref[rows, hcols[h]], cos, sin) * (d ** -0.5) for h in heads]
        qb = [_rope_full(q_ref[rows, hcols[h]], cos, sin).astype(BF16) for h in heads]
        kb = [k[h].astype(BF16) for h in heads]
        vb = [v_ref[rows, hcols[h]].astype(BF16) for h in heads]
        scores = [_dot_nt(qb[h], kb[h]) for h in heads]
        cross = [_dot(qb[h], state[h].astype(BF16)) for h in heads]
        kz_t = [(k[h] * zeta_ref[h]).T.astype(BF16) for h in heads]
        inner = [_dot((scores[h] * decay_ref[h]).astype(BF16), vb[h]) for h in heads]
        update = [_dot(kz_t[h], vb[h]) for h in heads]
        for h in heads:
            state[h] = state[h] * gamma_ref[h, 0:1, :] + update[h]
            ret = inner[h] + cross[h] * xi_ref[h]
            ms = jnp.mean(ret * ret, axis=-1, keepdims=True)
            y = ret * lax.rsqrt(ms + NORM_EPS) * gain_ref[h]
            o_ref[rows, hcols[h]] = (y * _silu(gate_ref[rows, hcols[h]])).astype(BF16)
    for h in heads:
        state_ref[h] = state[h]


def retention(proj, cos, sin, decay, zeta, xi, gamma, gain, b, t, rows=512):
    n = b * t
    steps = t // rows
    b_blk = BLK_QB // B_HEADS

    def cur(off):
        return pl.BlockSpec((rows, B_WIDTH), lambda i, s: (i * steps + s, b_blk + off))

    def table(shape):
        return pl.BlockSpec((B_HEADS,) + shape, lambda i, s: (0, 0, 0))

    return pl.pallas_call(
        _retention_kernel,
        grid=(b, steps),
        in_specs=[
            cur(0), cur(1), cur(2), cur(3),
            pl.BlockSpec((rows, HEAD_DIM), lambda i, s: (s, 0)),
            pl.BlockSpec((rows, HEAD_DIM), lambda i, s: (s, 0)),
            table((RET_CHUNK, RET_CHUNK)), table((RET_CHUNK, HEAD_DIM)), table((RET_CHUNK, HEAD_DIM)),
            table((8, HEAD_DIM)), table((1, HEAD_DIM)),
        ],
        out_specs=pl.BlockSpec((rows, B_WIDTH), lambda i, s: (i * steps + s, 0)),
        out_shape=jax.ShapeDtypeStruct((n, B_WIDTH), BF16),
        scratch_shapes=[pltpu.VMEM((B_HEADS, HEAD_DIM, HEAD_DIM), F32)],
        compiler_params=_cparams(("parallel", "arbitrary")),
        name="retention",
    )(proj, proj, proj, proj, cos, sin, decay, zeta, xi, gamma, gain)


def _retention_tables():
    c = RET_CHUNK
    log_gamma = jnp.log1p(-jnp.exp2(-5.0 - jnp.arange(B_HEADS, dtype=F32)))
    i = jnp.arange(c, dtype=F32)
    lg = log_gamma[:, None]
    tril = jnp.tril(jnp.ones((c, c), dtype=bool))
    rel = i[:, None] - i[None, :]
    decay = jnp.where(tril, jnp.exp(jnp.where(tril, lg[:, :, None] * rel, 0.0)), 0.0)
    zeta = jnp.exp(lg * (c - 1 - i))
    xi = jnp.exp(lg * (i + 1.0))
    gamma_c = jnp.exp(log_gamma * c)
    bcast = lambda v: jnp.broadcast_to(v[:, :, None], (B_HEADS, c, HEAD_DIM))
    gamma = jnp.broadcast_to(gamma_c[:, None, None], (B_HEADS, 8, HEAD_DIM))
    return decay, bcast(zeta), bcast(xi), gamma


def _lane_column(x, idx):
    lane = lax.broadcasted_iota(jnp.int32, x.shape, 1)
    col = jnp.sum(jnp.where(lane == idx, x, 0.0), axis=-1, keepdims=True)
    return jnp.broadcast_to(col, x.shape)


def _gdn_kernel(q_ref, k_ref, v_ref, qp_ref, kp_ref, vp_ref, z_ref, ba_ref, cw_ref, alog_ref, dtb_ref,
                gain_ref, o_ref, state_ref):
    r = GDN_GROUP
    c = GDN_CHUNK
    d = HEAD_DIM
    w = C_WIDTH
    hi = lax.Precision.HIGHEST
    first = pl.program_id(1) == 0

    @pl.when(first)
    def _():
        state_ref[...] = jnp.zeros_like(state_ref)

    row = lax.broadcasted_iota(jnp.int32, (r, r), 0)
    col = lax.broadcasted_iota(jnp.int32, (r, r), 1)
    same_chunk = (row // c) == (col // c)
    tril = same_chunk & (row >= col)
    strict = same_chunk & (row > col)
    lane_r = lax.broadcasted_iota(jnp.int32, (d, r), 1)

    def conv_silu(x_ref, prev_ref, w0):
        prev = jnp.where(first, 0.0, prev_ref[...])
        xe = jnp.concatenate([prev, x_ref[...]], axis=0)
        y = xe[8:, :] * cw_ref[CONV_WIDTH - 1:CONV_WIDTH, w0:w0 + w]
        for back in range(1, CONV_WIDTH):
            tap = CONV_WIDTH - 1 - back
            y = y + pltpu.roll(xe, back, 0)[8:, :] * cw_ref[tap:tap + 1, w0:w0 + w]
        return _silu(y)

    xq = conv_silu(q_ref, qp_ref, 0)
    xk = conv_silu(k_ref, kp_ref, w)
    xv = conv_silu(v_ref, vp_ref, 2 * w)

    ba = ba_ref[...]
    beta_all = _sigmoid(ba)
    pre = ba + dtb_ref[...]
    softplus = jnp.maximum(pre, 0.0) + jnp.log(1.0 + jnp.exp(-jnp.abs(pre)))
    g_all = -jnp.exp(alog_ref[...]) * softplus
    gcum_all = _dot(jnp.where(tril, 1.0, 0.0), g_all, precision=hi)
    gtot_all = _dot(jnp.where(same_chunk, 1.0, 0.0), g_all, precision=hi)

    heads = range(C_HEADS)
    hcols = [slice(h * d, (h + 1) * d) for h in heads]
    q, k, k_beta, kb16, lmask, exp_g, gcum, gtot, x = ([None] * C_HEADS for _ in range(9))
    for h in heads:
        xqh, xkh, v = xq[:, hcols[h]], xk[:, hcols[h]], xv[:, hcols[h]]
        q[h] = xqh * lax.rsqrt(jnp.sum(xqh * xqh, axis=-1, keepdims=True) + NORM_EPS) * (d ** -0.5)
        k[h] = xkh * lax.rsqrt(jnp.sum(xkh * xkh, axis=-1, keepdims=True) + NORM_EPS)
        beta = _lane_column(beta_all, h)
        gcum[h] = _lane_column(gcum_all, C_HEADS + h)
        gtot[h] = _lane_column(gtot_all, C_HEADS + h)
        gcum_col = jnp.concatenate([gcum[h], gcum[h]], axis=1)
        diff = gcum_col - gcum_col.T
        lmask[h] = jnp.where(tril, jnp.exp(jnp.where(tril, diff, 0.0)), 0.0)
        exp_g[h] = jnp.exp(gcum[h])
        k_beta[h] = k[h] * beta
        kb16[h] = k[h].astype(BF16)
        x[h] = jnp.concatenate([v * beta, k_beta[h] * exp_g[h]], axis=1)

    kk = [_dot_nt(k_beta[h].astype(BF16), kb16[h]) for h in heads]
    qk = [_dot_nt(q[h].astype(BF16), kb16[h]) for h in heads]
    p = [-jnp.where(strict, kk[h] * lmask[h], 0.0) for h in heads]
    attn = [(qk[h] * lmask[h]).astype(BF16) for h in heads]

    steps = int(np.log2(c))
    for s in range(steps):
        p16 = [p[h].astype(BF16) for h in heads]
        x = [x[h] + _dot(p16[h], x[h].astype(BF16)) for h in heads]
        if s + 1 < steps:
            p = [_dot(p16[h], p16[h]) for h in heads]

    u = [x[h][:, :d] for h in heads]
    w16 = [x[h][:, d:].astype(BF16) for h in heads]
    qdec16 = [(q[h] * exp_g[h]).astype(BF16) for h in heads]
    kdec_t = [(k[h] * jnp.exp(gtot[h] - gcum[h])).T for h in heads]
    chunk_decay = [jnp.exp(gtot[h]) for h in heads]

    state = [state_ref[h] for h in heads]
    vnew = [[] for _ in heads]
    for n in range(r // c):
        rows = slice(n * c, (n + 1) * c)
        s16 = [state[h].astype(BF16) for h in heads]
        ws = [_dot(w16[h][rows, :], s16[h]) for h in heads]
        qs = [_dot(qdec16[h][rows, :], s16[h]) for h in heads]
        vfull = []
        for h in heads:
            vnew[h].append((u[h][rows, :] - ws[h]).astype(BF16))
            vfull.append(jnp.concatenate(vnew[h] + [jnp.zeros((c, d), BF16)] * (r // c - 1 - n), axis=0))
        av = [_dot(attn[h][rows, :], vfull[h]) for h in heads]
        kd = [jnp.where((lane_r // c) == n, kdec_t[h], 0.0).astype(BF16) for h in heads]
        kv = [_dot(kd[h], vfull[h]) for h in heads]
        for h in heads:
            state[h] = state[h] * chunk_decay[h][n * c:n * c + 1, :] + kv[h]
            o = qs[h] + av[h]
            ms = jnp.mean(o * o, axis=-1, keepdims=True)
            y = o * lax.rsqrt(ms + NORM_EPS) * gain_ref[...]
            o_ref[rows, hcols[h]] = (y * _silu(z_ref[rows, hcols[h]])).astype(BF16)
    for h in heads:
        state_ref[h] = state[h]


def gated_deltanet(proj, conv_w, alog, dtb, gain, b, t):
    n = b * t
    r = GDN_GROUP
    groups = t // r
    c_blk = BLK_QC // C_HEADS

    def cur(off):
        return pl.BlockSpec((r, C_WIDTH), lambda i, g: (i * groups + g, c_blk + off))

    def prev(off):
        return pl.BlockSpec((8, C_WIDTH), lambda i, g: (jnp.maximum((i * groups + g) * (r // 8) - 1, 0), c_blk + off))

    vec = pl.BlockSpec((1, HEAD_DIM), lambda i, g: (0, 0))
    return pl.pallas_call(
        _gdn_kernel,
        grid=(b, groups),
        in_specs=[
            cur(0), cur(1), cur(2), prev(0), prev(1), prev(2), cur(3),
            pl.BlockSpec((r, HEAD_DIM), lambda i, g: (i * groups + g, BLK_BA)),
            pl.BlockSpec((CONV_WIDTH, 3 * C_WIDTH), lambda i, g: (0, 0)),
            vec, vec, vec,
        ],
        out_specs=pl.BlockSpec((r, C_WIDTH), lambda i, g: (i * groups + g, 0)),
        out_shape=jax.ShapeDtypeStruct((n, C_WIDTH), BF16),
        scratch_shapes=[pltpu.VMEM((C_HEADS, HEAD_DIM, HEAD_DIM), F32)],
        compiler_params=_cparams(("parallel", "arbitrary")),
        name="gated_deltanet",
    )(proj, proj, proj, proj, proj, proj, proj, proj, conv_w, alog, dtb, gain)


def _pad_lanes(v, fill=0.0):
    return jnp.concatenate([v, jnp.full((HEAD_DIM - v.shape[0],), fill, F32)])[None, :]


def kernel(x, attn_norm, w_in, q_norm, k_norm, ret_norm, conv_w, a_log, dt_bias, gdn_norm,
           w_branch, w_out, ffn_norm, w_gate, w_up, w_down):
    b, t, d = x.shape
    n = b * t
    depth = w_in.shape[0]
    xf = x.reshape(n, d)

    rope_freq = ROPE_THETA ** (-jnp.arange(0, ROPE_DIM, 2, dtype=F32) / ROPE_DIM)
    ret_freq = RET_THETA ** (-jnp.linspace(0.0, 1.0, HEAD_DIM // 2, dtype=F32))
    cos_a, sin_a = _rotary_tables(t, rope_freq, ROPE_DIM)
    cos_b, sin_b = _rotary_tables(t, ret_freq, HEAD_DIM)
    decay, zeta, xi, gamma = _retention_tables()
    w_proj = w_in_prep(w_in)

    for layer in range(depth):
        gates, h = in_proj_gates(xf, attn_norm[layer][None, :], w_proj, layer)
        proj = in_proj_main(h, w_proj, layer)

        k3, vt3, kmean = moba_kprep(proj, k_norm[layer][None, :], cos_a, sin_a, b, t)
        o_a = moba_attention(proj, q_norm[layer][None, :], cos_a, sin_a, k3, vt3,
                             kmean, b, t)

        o_b = retention(proj, cos_b, sin_b, decay, zeta, xi, gamma,
                        ret_norm[layer].reshape(B_HEADS, 1, HEAD_DIM), b, t)

        alog_v = jnp.concatenate([jnp.zeros((C_HEADS,), F32), a_log[layer]])
        dtb_v = jnp.concatenate([jnp.zeros((C_HEADS,), F32), dt_bias[layer]])
        o_c = gated_deltanet(proj, conv_w[layer], _pad_lanes(alog_v), _pad_lanes(dtb_v),
                             gdn_norm[layer][None, :], b, t)

        wbr = w_branch[layer].astype(BF16)
        merged = merge_branches(o_a, o_b, o_c, wbr[:A_WIDTH], wbr[A_WIDTH:A_WIDTH + B_WIDTH],
                                wbr[A_WIDTH + B_WIDTH:], gates)
        xf = residual_matmul(merged, w_out[layer].astype(BF16), xf, 1024, 1024, "out_proj")

        hidden = ffn_up(xf, ffn_norm[layer][None, :], w_gate[layer].astype(BF16), w_up[layer].astype(BF16))
        xf = residual_matmul(hidden, w_down[layer].astype(BF16), xf, 1024, 512, "ffn_down")
    return xf.reshape(b, t, d)
```
